```python
import math
import jax, jax.numpy as jnp
from jax import lax
import numpy as np


D_MODEL = 2048
BATCH = 2
SEQ = 16384
DEPTH = 1

HY_WIDTH = D_MODEL // 2
ATT_WIDTH = D_MODEL - HY_WIDTH
HY_ORDER = 2
HY_SHORT = 3
HY_EMB = 33
HY_BANDS = (HY_EMB - 1) // 2
HY_FFN = 64
HY_DIRS = 2
HY_TARGET = 1e-2
HY_FAST_DECAY = 0.3
HY_SLOW_DECAY = 1.5
HEAD_DIM = 128
N_HEADS = ATT_WIDTH // HEAD_DIM
N_KV_HEADS = 2
Q_PER_KV = N_HEADS // N_KV_HEADS
KV_WIDTH = N_KV_HEADS * HEAD_DIM
ROPE_THETA = 10000.0
ROPE_AXIS_DIM = HEAD_DIM // 2
GRID_W = 64
Q_BLOCK = 128
IN_WIDTH = (HY_ORDER + 1) * HY_WIDTH + ATT_WIDTH + 2 * KV_WIDTH
N_EXPERTS = 16
EXPERT_FF = 1536
CAPACITY_FACTOR = 2
NORM_EPS = 1e-6
DN_ALPHA = (2 * DEPTH) ** 0.25
DN_BETA = (8 * DEPTH) ** -0.25

kernel_name = "hymba_hyena_gqa_axial_ec_moe_deepnorm"


def layer_norm(x, g, b):
    xf = x.astype(jnp.float32)
    mu = jnp.mean(xf, axis=-1, keepdims=True)
    xc = xf - mu
    var = jnp.mean(xc * xc, axis=-1, keepdims=True)
    return (xc * lax.rsqrt(var + NORM_EPS) * g + b).astype(x.dtype)


def rms_norm(x, g):
    xf = x.astype(jnp.float32)
    return xf * lax.rsqrt(jnp.mean(xf * xf, axis=-1, keepdims=True) + NORM_EPS) * g


def short_conv(u, w, b):
    L = u.shape[1]
    pad = HY_SHORT // 2
    up = jnp.pad(u, ((0, 0), (pad, pad), (0, 0)))
    return sum(w[j] * up[:, j:j + L] for j in range(HY_SHORT)) + b


def hyena_filters(L, w1, b1, f1, w2, b2, f2, w3, decay):
    t = jnp.linspace(0.0, 1.0, L, dtype=jnp.float32)[:, None]
    w = 2.0 * math.pi * jnp.arange(L, dtype=jnp.float32)[:, None] / L
    f = jnp.linspace(1e-4, HY_BANDS - 1, HY_BANDS, dtype=jnp.float32)[None, :]
    emb = jnp.concatenate([t, jnp.cos(f * w), -jnp.sin(f * w)], axis=-1)
    h = jnp.sin(f1 * (emb @ w1 + b1))
    h = jnp.sin(f2 * (h @ w2 + b2))
    h = (h @ w3).astype(jnp.float32).reshape(L, HY_DIRS, HY_ORDER, HY_WIDTH)
    window = jnp.exp(-t.reshape(L, 1, 1, 1) * jnp.abs(decay.astype(jnp.float32)))
    return h * window


def bidir_fftconv(u, h_fwd, h_bwd, d_skip):
    L = u.shape[1]
    n = 2 * L
    g = jnp.concatenate([h_fwd, jnp.zeros((1, h_fwd.shape[1]), jnp.float32),
                         jnp.flip(h_bwd[1:], axis=0)], axis=0)
    uf = u.astype(jnp.float32)
    spec = jnp.fft.rfft(uf, n=n, axis=1) * jnp.fft.rfft(g, n=n, axis=0)[None]
    y = jnp.fft.irfft(spec, n=n, axis=1)[:, :L]
    return y + uf * d_skip.astype(jnp.float32)


def axial_rope(L):
    rows = L // GRID_W
    row = jnp.repeat(jnp.arange(rows, dtype=jnp.float32), GRID_W)
    col = jnp.tile(jnp.arange(GRID_W, dtype=jnp.float32), rows)
    inv = 1.0 / (ROPE_THETA ** (jnp.arange(0, ROPE_AXIS_DIM, 2, dtype=jnp.float32) / ROPE_AXIS_DIM))
    ang = jnp.concatenate([row[:, None] * inv, col[:, None] * inv], axis=-1)
    return jnp.cos(ang), jnp.sin(ang)


def apply_rope(x, cos, sin):
    xr = x.reshape(x.shape[:-1] + (HEAD_DIM // 2, 2))
    x0, x1 = xr[..., 0], xr[..., 1]
    c = cos[None, :, None, :]
    s = sin[None, :, None, :]
    return jnp.stack([x0 * c - x1 * s, x0 * s + x1 * c], axis=-1).reshape(x.shape)


def block_attention(q, k, v):
    B, L = q.shape[:2]
    nb = L // Q_BLOCK
    qb = q.reshape(B, nb, Q_BLOCK, N_KV_HEADS, Q_PER_KV, HEAD_DIM).transpose(1, 0, 2, 3, 4, 5)
    scale = HEAD_DIM ** -0.5

    def one_block(qi):
        s = jnp.einsum('bqhgd,bkhd->bhgqk', qi, k) * scale
        p = jax.nn.softmax(s, axis=-1)
        return jnp.einsum('bhgqk,bkhd->bqhgd', p, v)

    o = lax.map(one_block, qb)
    return o.transpose(1, 0, 2, 3, 4, 5).reshape(B, L, N_HEADS * HEAD_DIM)


def expert_choice_moe(x, w_router, b_router, w_gate, w_up, w_down):
    B, L, _ = x.shape
    C = CAPACITY_FACTOR * L // N_EXPERTS
    logits = jnp.einsum('bld,de->ble', x, w_router).astype(jnp.float32) + b_router
    aff = jax.nn.softmax(logits, axis=-1)
    gate, idx = lax.top_k(aff.transpose(0, 2, 1), C)
    bi = jnp.arange(B)[:, None, None]
    xg = x[bi, idx]
    h = jax.nn.silu(jnp.einsum('becd,edf->becf', xg, w_gate)) * jnp.einsum('becd,edf->becf', xg, w_up)
    y = jnp.einsum('becf,efd->becd', h, w_down) * gate[..., None]
    return jnp.zeros_like(x).at[bi, idx].add(y.astype(x.dtype))


def setup_inputs(seed: int = 0) -> dict:
    key = jax.random.key(seed)
    ks = jax.random.split(key, 32)
    f32 = jnp.float32
    nrm = lambda k, shape, s: jax.random.normal(k, shape, f32) * s
    x = nrm(ks[0], (BATCH, SEQ, D_MODEL), 1.0)
    w_in = nrm(ks[1], (D_MODEL, IN_WIDTH), D_MODEL ** -0.5)
    w_in = w_in.at[:, IN_WIDTH - KV_WIDTH:].multiply(DN_BETA)
    b_in = nrm(ks[2], (IN_WIDTH,), 0.01)
    hy_conv_w = nrm(ks[3], (HY_SHORT, (HY_ORDER + 1) * HY_WIDTH), HY_SHORT ** -0.5)
    hy_conv_b = nrm(ks[4], ((HY_ORDER + 1) * HY_WIDTH,), 0.01)
    hy_ffn_w1 = nrm(ks[5], (HY_EMB, HY_FFN), HY_EMB ** -0.5)
    hy_ffn_b1 = nrm(ks[6], (HY_FFN,), 0.1)
    hy_sin_f1 = 1.0 + nrm(ks[7], (HY_FFN,), 0.05)
    hy_ffn_w2 = nrm(ks[8], (HY_FFN, HY_FFN), HY_FFN ** -0.5)
    hy_ffn_b2 = nrm(ks[9], (HY_FFN,), 0.1)
    hy_sin_f2 = 1.0 + nrm(ks[10], (HY_FFN,), 0.05)
    hy_ffn_w3 = nrm(ks[11], (HY_FFN, HY_DIRS * HY_ORDER * HY_WIDTH), HY_FFN ** -0.5)
    min_rate = -math.log(HY_TARGET) / HY_SLOW_DECAY
    max_rate = -math.log(HY_TARGET) / HY_FAST_DECAY
    base = jnp.linspace(min_rate, max_rate, HY_WIDTH, dtype=f32)
    hy_decay = base * (1.0 + nrm(ks[12], (HY_DIRS, HY_ORDER, HY_WIDTH), 0.05))
    hy_skip = nrm(ks[13], (HY_ORDER, HY_WIDTH), 1.0)
    q_norm = 1.0 + nrm(ks[14], (HEAD_DIM,), 0.02)
    k_norm = 1.0 + nrm(ks[15], (HEAD_DIM,), 0.02)
    g_hy = 1.0 + nrm(ks[16], (HY_WIDTH,), 0.02)
    g_attn = 1.0 + nrm(ks[17], (ATT_WIDTH,), 0.02)
    w_out = nrm(ks[18], (D_MODEL, D_MODEL), D_MODEL ** -0.5 * DN_BETA)
    ln1_g = 1.0 + nrm(ks[19], (D_MODEL,), 0.02)
    ln1_b = nrm(ks[20], (D_MODEL,), 0.01)
    w_router = nrm(ks[21], (D_MODEL, N_EXPERTS), D_MODEL ** -0.5)
    b_router = nrm(ks[22], (N_EXPERTS,), 0.01)
    w_gate = nrm(ks[23], (N_EXPERTS, D_MODEL, EXPERT_FF), D_MODEL ** -0.5)
    w_up = nrm(ks[24], (N_EXPERTS, D_MODEL, EXPERT_FF), D_MODEL ** -0.5)
    w_down = nrm(ks[25], (N_EXPERTS, EXPERT_FF, D_MODEL), EXPERT_FF ** -0.5 * DN_BETA)
    ln2_g = 1.0 + nrm(ks[26], (D_MODEL,), 0.02)
    ln2_b = nrm(ks[27], (D_MODEL,), 0.01)
    return {"x": x, "w_in": w_in, "b_in": b_in, "hy_conv_w": hy_conv_w, "hy_conv_b": hy_conv_b,
            "hy_ffn_w1": hy_ffn_w1, "hy_ffn_b1": hy_ffn_b1, "hy_sin_f1": hy_sin_f1,
            "hy_ffn_w2": hy_ffn_w2, "hy_ffn_b2": hy_ffn_b2, "hy_sin_f2": hy_sin_f2,
            "hy_ffn_w3": hy_ffn_w3, "hy_decay": hy_decay, "hy_skip": hy_skip,
            "q_norm": q_norm, "k_norm": k_norm, "g_hy": g_hy, "g_attn": g_attn,
            "w_out": w_out, "ln1_g": ln1_g, "ln1_b": ln1_b, "w_router": w_router,
            "b_router": b_router, "w_gate": w_gate, "w_up": w_up, "w_down": w_down,
            "ln2_g": ln2_g, "ln2_b": ln2_b}


def reference(x, w_in, b_in, hy_conv_w, hy_conv_b, hy_ffn_w1, hy_ffn_b1, hy_sin_f1,
              hy_ffn_w2, hy_ffn_b2, hy_sin_f2, hy_ffn_w3, hy_decay, hy_skip,
              q_norm, k_norm, g_hy, g_attn, w_out, ln1_g, ln1_b, w_router, b_router,
              w_gate, w_up, w_down, ln2_g, ln2_b):
    B, L, _ = x.shape
    f32 = jnp.float32
    for _layer in range(DEPTH):
        proj = jnp.einsum('bld,de->ble', x, w_in) + b_in
        s1 = (HY_ORDER + 1) * HY_WIDTH
        s2 = s1 + ATT_WIDTH
        s3 = s2 + KV_WIDTH
        hy_in = short_conv(proj[..., :s1], hy_conv_w, hy_conv_b)
        hv, hx1, hx2 = jnp.split(hy_in, HY_ORDER + 1, axis=-1)
        filt = hyena_filters(L, hy_ffn_w1, hy_ffn_b1, hy_sin_f1, hy_ffn_w2, hy_ffn_b2,
                             hy_sin_f2, hy_ffn_w3, hy_decay)
        z = hx1.astype(f32) * bidir_fftconv(hv, filt[:, 0, 0], filt[:, 1, 0], hy_skip[0])
        y_hy = hx2.astype(f32) * bidir_fftconv(z, filt[:, 0, 1], filt[:, 1, 1], hy_skip[1])
        q = rms_norm(proj[..., s1:s2].reshape(B, L, N_HEADS, HEAD_DIM), q_norm)
        k = rms_norm(proj[..., s2:s3].reshape(B, L, N_KV_HEADS, HEAD_DIM), k_norm)
        v = proj[..., s3:].reshape(B, L, N_KV_HEADS, HEAD_DIM).astype(f32)
        cos, sin = axial_rope(L)
        y_att = block_attention(apply_rope(q, cos, sin), apply_rope(k, cos, sin), v)
        mixed = jnp.concatenate([rms_norm(y_hy, g_hy), rms_norm(y_att, g_attn)], axis=-1).astype(x.dtype)
        x = layer_norm(DN_ALPHA * x + jnp.einsum('ble,ed->bld', mixed, w_out), ln1_g, ln1_b)
        x = layer_norm(DN_ALPHA * x + expert_choice_moe(x, w_router, b_router, w_gate, w_up, w_down),
                       ln2_g, ln2_b)
    return x
```

```python
import functools
import math

import numpy as np
import jax
import jax.numpy as jnp
from jax import lax
from jax.experimental import pallas as pl
from jax.experimental.pallas import tpu as pltpu

F32 = jnp.float32
BF16 = jnp.bfloat16
I32 = jnp.int32
HIGHEST = lax.Precision.HIGHEST

HY_WIDTH = 1024
HY_BANDS = 16
HEAD_DIM = 128
N_HEADS = 8
N_KV_HEADS = 2
Q_PER_KV = N_HEADS // N_KV_HEADS
ROPE_THETA = 10000.0
GRID_W = 64
N_EXPERTS = 16
CAPACITY_FACTOR = 2
NORM_EPS = 1e-6
DN_ALPHA = 2.0 ** 0.25

LANES = 128
SUBLANES = 8
BF16_ROWS = 16
FFT_N1 = 128
VMEM_LIMIT = 56 * 1024 * 1024


def _cparams(n_axes, vmem=VMEM_LIMIT):
    return pltpu.CompilerParams(dimension_semantics=("arbitrary",) * n_axes, vmem_limit_bytes=vmem)


def _round_up(a, m):
    return (a + m - 1) // m * m


def _inproj_kernel(x_ref, w_ref, b_ref, o_ref, xb_ref):
    @pl.when(pl.program_id(1) == 0)
    def _():
        xb_ref[...] = x_ref[...].astype(BF16)

    o_ref[...] = jnp.dot(xb_ref[...], w_ref[...], preferred_element_type=F32) + b_ref[...]


def _in_projection(x2, w_bf, b_row):
    n, d = x2.shape
    wdt = w_bf.shape[1]
    tm = min(1024, n)
    tn = 512
    return pl.pallas_call(
        _inproj_kernel,
        grid=(n // tm, wdt // tn),
        in_specs=[pl.BlockSpec((tm, d), lambda i, j: (i, 0)),
                  pl.BlockSpec((d, tn), lambda i, j: (0, j)),
                  pl.BlockSpec((1, tn), lambda i, j: (0, j))],
        out_specs=pl.BlockSpec((tm, tn), lambda i, j: (i, j)),
        out_shape=jax.ShapeDtypeStruct((n, wdt), F32),
        scratch_shapes=[pltpu.VMEM((tm, d), BF16)],
        compiler_params=_cparams(2),
        name="in_projection",
    )(x2, w_bf, b_row)


def _shortconv_kernel(*refs, tt, seq):
    mains, prevs, nexts = refs[0:3], refs[3:6], refs[6:9]
    w_refs, b_refs, outs = refs[9:12], refs[12:15], refs[15:18]
    i = pl.program_id(0)
    r_in = lax.broadcasted_iota(I32, (tt, 1), 0)
    tpos = (i * tt + r_in) % seq
    for m, p, nx, w, b, o in zip(mains, prevs, nexts, w_refs, b_refs, outs):
        cur = m[...]
        up = pltpu.roll(cur, 1, axis=0)
        up = jnp.where(r_in == 0, p[SUBLANES - 1:SUBLANES, :], up)
        up = jnp.where(tpos == 0, 0.0, up)
        dn = pltpu.roll(cur, tt - 1, axis=0)
        dn = jnp.where(r_in == tt - 1, nx[0:1, :], dn)
        dn = jnp.where(tpos == seq - 1, 0.0, dn)
        wv = w[...]
        o[...] = wv[0:1] * up + wv[1:2] * cur + wv[2:3] * dn + b[...]


def _short_conv(proj, conv_w, conv_b, seq):
    n = proj.shape[0]
    tt = min(512, seq)
    tc = 512
    ncb = HY_WIDTH // tc
    nrb = n // SUBLANES
    hb = tt // SUBLANES
    main = [pl.BlockSpec((tt, tc), functools.partial(lambda i, j, p: (i, p * ncb + j), p=p)) for p in range(3)]
    prev = [pl.BlockSpec((SUBLANES, tc),
                         functools.partial(lambda i, j, p: (jnp.maximum(i * hb - 1, 0), p * ncb + j), p=p))
            for p in range(3)]
    nxt = [pl.BlockSpec((SUBLANES, tc),
                        functools.partial(lambda i, j, p: (jnp.minimum((i + 1) * hb, nrb - 1), p * ncb + j), p=p))
           for p in range(3)]
    wsp = [pl.BlockSpec((3, tc), functools.partial(lambda i, j, p: (0, p * ncb + j), p=p)) for p in range(3)]
    bsp = [pl.BlockSpec((1, tc), functools.partial(lambda i, j, p: (0, p * ncb + j), p=p)) for p in range(3)]
    osp = [pl.BlockSpec((tt, tc), lambda i, j: (i, j)) for _ in range(3)]
    return pl.pallas_call(
        functools.partial(_shortconv_kernel, tt=tt, seq=seq),
        grid=(n // tt, ncb),
        in_specs=main + prev + nxt + wsp + bsp,
        out_specs=osp,
        out_shape=[jax.ShapeDtypeStruct((n, HY_WIDTH), F32)] * 3,
        compiler_params=_cparams(2),
        name="short_conv",
    )(proj, proj, proj, proj, proj, proj, proj, proj, proj,
      conv_w, conv_w, conv_w, conv_b, conv_b, conv_b)


def _filter_kernel(freq_ref, w1_ref, b1_ref, f1_ref, w2_ref, b2_ref, f2_ref, w3_ref, dec_ref, o_ref, *, tt, seq):
    i = pl.program_id(0)
    r = (i * tt + lax.broadcasted_iota(I32, (tt, LANES), 0)).astype(F32)
    lane = lax.broadcasted_iota(I32, (tt, LANES), 1)
    t = r / float(seq - 1)
    w = (2.0 * math.pi / seq) * r
    arg = freq_ref[...] * w
    emb = jnp.where(lane == 0, t,
                    jnp.where(lane <= HY_BANDS, jnp.cos(arg),
                              jnp.where(lane <= 2 * HY_BANDS, -jnp.sin(arg), 0.0)))
    h = jnp.sin(f1_ref[...] * (jnp.dot(emb, w1_ref[...], preferred_element_type=F32, precision=HIGHEST)
                               + b1_ref[...]))
    h = jnp.sin(f2_ref[...] * (jnp.dot(h, w2_ref[...], preferred_element_type=F32, precision=HIGHEST)
                               + b2_ref[...]))
    h3 = jnp.dot(h, w3_ref[...], preferred_element_type=F32, precision=HIGHEST)
    win = jnp.exp(-t[:, 0:1] * jnp.abs(dec_ref[...]))
    out = h3 * win
    col = lax.broadcasted_iota(I32, out.shape, 1)
    row = i * tt + lax.broadcasted_iota(I32, out.shape, 0)
    o_ref[...] = jnp.where((row == 0) & (col >= out.shape[1] // 2), 0.0, out)


def _hyena_filters(seq, w1, b1, f1, w2, b2, f2, w3, decay):
    ffn = w1.shape[1]
    nout = w3.shape[1]
    tt = min(256, seq)
    band = jnp.linspace(1e-4, HY_BANDS - 1, HY_BANDS, dtype=F32)
    freq = jnp.zeros((1, LANES), F32).at[0, 1:1 + HY_BANDS].set(band).at[0, 1 + HY_BANDS:1 + 2 * HY_BANDS].set(band)
    w1p = jnp.zeros((LANES, ffn), F32).at[:w1.shape[0]].set(w1)
    full = lambda shape: pl.BlockSpec(shape, lambda i: (0,) * len(shape))
    return pl.pallas_call(
        functools.partial(_filter_kernel, tt=tt, seq=seq),
        grid=(seq // tt,),
        in_specs=[full((1, LANES)), full((LANES, ffn)), full((1, ffn)), full((1, ffn)),
                  full((ffn, ffn)), full((1, ffn)), full((1, ffn)), full((ffn, nout)), full((1, nout))],
        out_specs=pl.BlockSpec((tt, nout), lambda i: (i, 0)),
        out_shape=jax.ShapeDtypeStruct((seq, nout), F32),
        compiler_params=_cparams(1),
        name="hyena_filters",
    )(freq, w1p, b1.reshape(1, ffn), f1.reshape(1, ffn), w2, b2.reshape(1, ffn), f2.reshape(1, ffn),
      w3, decay.reshape(1, nout))


def _fft_tables(seq):
    n2h = seq // FFT_N1
    n2 = 2 * n2h
    n = FFT_N1 * n2
    k2n = n2h + 1
    k2p = _round_up(k2n, BF16_ROWS)
    kk = np.arange(k2p)[:, None]
    nn = np.arange(n2h)[None, :]
    ang = 2.0 * np.pi * ((kk * nn) % n2) / n2
    valid = (kk < k2n).astype(np.float64)
    fa = np.concatenate([np.cos(ang) * valid, -np.sin(ang) * valid], axis=0)
    wk = np.where((kk == 0) | (kk == n2h), 1.0, 2.0) * valid
    gc = np.concatenate([(np.cos(ang) * wk).T, (-np.sin(ang) * wk).T], axis=1)
    n1 = np.arange(FFT_N1)[None, :]
    tw_ang = 2.0 * np.pi * ((kk * n1) % n) / n
    twr = (np.cos(tw_ang) * valid).reshape(k2p, 1, FFT_N1)
    twi = (-np.sin(tw_ang) * valid).reshape(k2p, 1, FFT_N1)
    k1 = np.arange(FFT_N1)[:, None]
    base = 2.0 * np.pi * ((k1 * n1) % FFT_N1) / FFT_N1
    return dict(n2h=n2h, k2p=k2p, n=n,
                fa=jnp.asarray(fa, BF16), gc=jnp.asarray(gc, BF16),
                twr=jnp.asarray(twr, F32), twi=jnp.asarray(twi, F32),
                cr=jnp.asarray(np.cos(base), F32), ci=jnp.asarray(-np.sin(base), F32))


def _ffta_kernel(x_ref, fa_ref, o_ref):
    xb = x_ref[0].astype(BF16)
    o_ref[0] = jnp.dot(fa_ref[...], xb, preferred_element_type=F32).astype(BF16)


def _fft_stage_a(xv, fa):
    bx, n2h, cols = xv.shape
    rows = fa.shape[0]
    cb = min(2048, cols)
    return pl.pallas_call(
        _ffta_kernel,
        grid=(bx, cols // cb),
        in_specs=[pl.BlockSpec((1, n2h, cb), lambda b, j: (b, 0, j)),
                  pl.BlockSpec((rows, n2h), lambda b, j: (0, 0))],
        out_specs=pl.BlockSpec((1, rows, cb), lambda b, j: (b, 0, j)),
        out_shape=jax.ShapeDtypeStruct((bx, rows, cols), BF16),
        compiler_params=_cparams(2),
        name="fft_stage_a",
    )(xv, fa)


def _stage_b_matrix(twr_ref, twi_ref, cr_ref, ci_ref):
    tr = twr_ref[0]
    ti = twi_ref[0]
    cr = cr_ref[...]
    ci = ci_ref[...]
    mr = cr * tr - ci * ti
    mi = ci * tr + cr * ti
    return mr, mi


def _fftb_filter_kernel(a_ref, twr_ref, twi_ref, cr_ref, ci_ref, o_ref, *, scale):
    mr, mi = _stage_b_matrix(twr_ref, twi_ref, cr_ref, ci_ref)
    fb = jnp.concatenate([jnp.concatenate([mr, -mi], axis=1),
                          jnp.concatenate([mi, mr], axis=1)], axis=0).astype(BF16)
    a = jnp.concatenate([a_ref[0, 0, 0], a_ref[0, 1, 0]], axis=0)
    h = jnp.dot(fb, a, preferred_element_type=F32)
    half = h.shape[1] // 2
    hr, hi = h[:FFT_N1], h[FFT_N1:]
    o_ref[0, 0] = (hr[:, :half] + hr[:, half:]) * scale
    o_ref[0, 1] = (hi[:, :half] - hi[:, half:]) * scale


def _fft_stage_b_filter(a5, tabs, scale):
    k2p = tabs["k2p"]
    c4 = a5.shape[-1]
    return pl.pallas_call(
        functools.partial(_fftb_filter_kernel, scale=scale),
        grid=(k2p,),
        in_specs=[pl.BlockSpec((1, 2, 1, FFT_N1, c4), lambda k: (0, 0, k, 0, 0)),
                  pl.BlockSpec((1, 1, FFT_N1), lambda k: (k, 0, 0)),
                  pl.BlockSpec((1, 1, FFT_N1), lambda k: (k, 0, 0)),
                  pl.BlockSpec((FFT_N1, FFT_N1), lambda k: (0, 0)),
                  pl.BlockSpec((FFT_N1, FFT_N1), lambda k: (0, 0))],
        out_specs=pl.BlockSpec((1, 2, FFT_N1, c4 // 2), lambda k: (k, 0, 0, 0)),
        out_shape=jax.ShapeDtypeStruct((k2p, 2, FFT_N1, c4 // 2), F32),
        compiler_params=_cparams(1),
        name="fft_stage_b_filter",
    )(a5, tabs["twr"], tabs["twi"], tabs["cr"], tabs["ci"])


def _fftb_conv_kernel(a_ref, g_ref, twr_ref, twi_ref, cr_ref, ci_ref, o_ref):
    mr, mi = _stage_b_matrix(twr_ref, twi_ref, cr_ref, ci_ref)
    fb = jnp.concatenate([jnp.concatenate([mr, -mi], axis=1),
                          jnp.concatenate([mi, mr], axis=1)], axis=0).astype(BF16)
    mrt, mit = mr.T, mi.T
    fbt = jnp.concatenate([jnp.concatenate([mrt, mit], axis=1),
                           jnp.concatenate([-mit, mrt], axis=1)], axis=0).astype(BF16)
    gr = g_ref[0, 0]
    gi = g_ref[0, 1]
    for b in range(a_ref.shape[0]):
        a = jnp.concatenate([a_ref[b, 0, 0], a_ref[b, 1, 0]], axis=0)
        xh = jnp.dot(fb, a, preferred_element_type=F32)
        xr, xi = xh[:FFT_N1], xh[FFT_N1:]
        y = jnp.concatenate([xr * gr - xi * gi, xr * gi + xi * gr], axis=0).astype(BF16)
        c = jnp.dot(fbt, y, preferred_element_type=F32)
        o_ref[b, 0, 0] = c[:FFT_N1].astype(BF16)
        o_ref[b, 1, 0] = c[FFT_N1:].astype(BF16)


def _fft_stage_b_conv(a5, g, order, tabs):
    bsz, _, k2p, _, c = a5.shape
    return pl.pallas_call(
        _fftb_conv_kernel,
        grid=(k2p,),
        in_specs=[pl.BlockSpec((bsz, 2, 1, FFT_N1, c), lambda k: (0, 0, k, 0, 0)),
                  pl.BlockSpec((1, 2, FFT_N1, c), lambda k: (k, 0, 0, order)),
                  pl.BlockSpec((1, 1, FFT_N1), lambda k: (k, 0, 0)),
                  pl.BlockSpec((1, 1, FFT_N1), lambda k: (k, 0, 0)),
                  pl.BlockSpec((FFT_N1, FFT_N1), lambda k: (0, 0)),
                  pl.BlockSpec((FFT_N1, FFT_N1), lambda k: (0, 0))],
        out_specs=pl.BlockSpec((bsz, 2, 1, FFT_N1, c), lambda k: (0, 0, k, 0, 0)),
        out_shape=jax.ShapeDtypeStruct(a5.shape, BF16),
        compiler_params=_cparams(1),
        name="fft_stage_b_conv",
    )(a5, g, tabs["twr"], tabs["twi"], tabs["cr"], tabs["ci"])


def _fftc_kernel(c_ref, gc_ref, u_ref, gate_ref, d_ref, o_ref):
    y = jnp.dot(gc_ref[...], c_ref[0], preferred_element_type=F32)
    o_ref[0] = gate_ref[0] * (y + d_ref[...] * u_ref[0])


def _fft_stage_c(cm, gc, uv, gatev, d_blk):
    bsz, rows, cols = cm.shape
    n2h = uv.shape[1]
    cb = d_blk.shape[1]
    return pl.pallas_call(
        _fftc_kernel,
        grid=(bsz, cols // cb),
        in_specs=[pl.BlockSpec((1, rows, cb), lambda b, j: (b, 0, j)),
                  pl.BlockSpec((n2h, rows), lambda b, j: (0, 0)),
                  pl.BlockSpec((1, n2h, cb), lambda b, j: (b, 0, j)),
                  pl.BlockSpec((1, n2h, cb), lambda b, j: (b, 0, j)),
                  pl.BlockSpec((1, cb), lambda b, j: (0, 0))],
        out_specs=pl.BlockSpec((1, n2h, cb), lambda b, j: (b, 0, j)),
        out_shape=jax.ShapeDtypeStruct(uv.shape, F32),
        compiler_params=_cparams(2),
        name="fft_stage_c",
    )(cm, gc, uv, gatev, d_blk)


def _hyena_mix(hv, hx1, hx2, filt, skip, bsz, seq):
    c = hv.shape[1]
    tabs = _fft_tables(seq)
    n2h, k2p, n = tabs["n2h"], tabs["k2p"], tabs["n"]
    cols = FFT_N1 * c
    cb = min(2048, cols)
    fa_f = _fft_stage_a(filt.reshape(1, n2h, FFT_N1 * 4 * c), tabs["fa"])
    g = _fft_stage_b_filter(fa_f.reshape(1, 2, k2p, FFT_N1, 4 * c), tabs, 1.0 / n)

    def conv(u, gate, order):
        a = _fft_stage_a(u.reshape(bsz, n2h, cols), tabs["fa"])
        cm = _fft_stage_b_conv(a.reshape(bsz, 2, k2p, FFT_N1, c), g, order, tabs)
        d_blk = jnp.tile(skip[order].reshape(1, c), (1, cb // c))
        z = _fft_stage_c(cm.reshape(bsz, 2 * k2p, cols), tabs["gc"], u.reshape(bsz, n2h, cols),
                         gate.reshape(bsz, n2h, cols), d_blk)
        return z.reshape(bsz * seq, c)

    z = conv(hv, hx1, 0)
    return conv(z, hx2, 1)


def _qkprep_kernel(q_ref, k_ref, v_ref, qn_ref, kn_ref, oq_ref, ok_ref, ov_ref, *, tt, seq):
    i = pl.program_id(0)
    t = (i * tt + lax.broadcasted_iota(I32, (tt, HEAD_DIM), 0)) % seq
    lane = lax.broadcasted_iota(I32, (tt, HEAD_DIM), 1)
    j = lane % (HEAD_DIM // 2)
    quarter = HEAD_DIM // 4
    f = (j % quarter).astype(F32)
    inv = jnp.exp(f * (-2.0 * math.log(ROPE_THETA) / (HEAD_DIM // 2)))
    pos = jnp.where(j < quarter, t // GRID_W, t % GRID_W).astype(F32)
    ang = pos * inv
    cosv = jnp.cos(ang)
    sinv = jnp.sin(ang)
    sin_signed = jnp.where(lane < HEAD_DIM // 2, -sinv, sinv)

    def norm_rope(x, g, scale):
        ms = jnp.mean(x * x, axis=-1, keepdims=True)
        xn = x * lax.rsqrt(ms + NORM_EPS) * g
        return (xn * cosv + pltpu.roll(xn, HEAD_DIM // 2, axis=1) * sin_signed) * scale

    qg = qn_ref[...]
    kg = kn_ref[...]
    for h in range(N_HEADS):
        sl = slice(h * HEAD_DIM, (h + 1) * HEAD_DIM)
        oq_ref[:, sl] = norm_rope(q_ref[:, sl], qg, HEAD_DIM ** -0.5).astype(BF16)
    for h in range(N_KV_HEADS):
        sl = slice(h * HEAD_DIM, (h + 1) * HEAD_DIM)
        ok_ref[:, sl] = norm_rope(k_ref[:, sl], kg, 1.0).astype(BF16)
    ov_ref[...] = v_ref[...].astype(BF16)


def _qk_prep(proj, qn, kn, seq):
    n = proj.shape[0]
    tt = min(512, seq)
    qw = N_HEADS * HEAD_DIM
    kw = N_KV_HEADS * HEAD_DIM
    q_off = (3 * HY_WIDTH) // qw
    k_off = (3 * HY_WIDTH + qw) // kw
    return pl.pallas_call(
        functools.partial(_qkprep_kernel, tt=tt, seq=seq),
        grid=(n // tt,),
        in_specs=[pl.BlockSpec((tt, qw), lambda i: (i, q_off)),
                  pl.BlockSpec((tt, kw), lambda i: (i, k_off)),
                  pl.BlockSpec((tt, kw), lambda i: (i, k_off + 1)),
                  pl.BlockSpec((1, HEAD_DIM), lambda i: (0, 0)),
                  pl.BlockSpec((1, HEAD_DIM), lambda i: (0, 0))],
        out_specs=[pl.BlockSpec((tt, qw), lambda i: (i, 0)),
                   pl.BlockSpec((tt, kw), lambda i: (i, 0)),
                   pl.BlockSpec((tt, kw), lambda i: (i, 0))],
        out_shape=[jax.ShapeDtypeStruct((n, qw), BF16),
                   jax.ShapeDtypeStruct((n, kw), BF16),
                   jax.ShapeDtypeStruct((n, kw), BF16)],
        compiler_params=_cparams(1),
        name="qk_prep",
    )(proj, proj, proj, qn, kn)


def _flash_kernel(q_ref, k_ref, v_ref, o_ref, m_sc, l_sc, acc_sc, *, tq, tk, seq):
    q = jnp.concatenate([q_ref[0, :, g * HEAD_DIM:(g + 1) * HEAD_DIM] for g in range(Q_PER_KV)], axis=0)
    m_sc[...] = jnp.full(m_sc.shape, -jnp.inf, F32)
    l_sc[...] = jnp.zeros(l_sc.shape, F32)
    acc_sc[...] = jnp.zeros(acc_sc.shape, F32)

    def body(kk, carry):
        start = pl.multiple_of(kk * tk, tk)
        kc = k_ref[0, pl.ds(start, tk), :]
        vc = v_ref[0, pl.ds(start, tk), :]
        s = lax.dot_general(q, kc, (((1,), (1,)), ((), ())), preferred_element_type=F32)
        m_prev = m_sc[...]
        m_new = jnp.maximum(m_prev, jnp.max(s, axis=-1, keepdims=True))
        alpha = jnp.exp(m_prev - m_new)
        p = jnp.exp(s - m_new)
        l_sc[...] = alpha * l_sc[...] + jnp.sum(p, axis=-1, keepdims=True)
        acc_sc[...] = alpha * acc_sc[...] + jnp.dot(p.astype(BF16), vc, preferred_element_type=F32)
        m_sc[...] = m_new
        return carry

    lax.fori_loop(0, seq // tk, body, 0)
    o = acc_sc[...] / l_sc[...]
    for g in range(Q_PER_KV):
        o_ref[0, :, g * HEAD_DIM:(g + 1) * HEAD_DIM] = o[g * tq:(g + 1) * tq]


def _flash_attention(q, k, v, bsz, seq):
    tq = min(256, seq)
    tk = min(512, seq)
    gw = Q_PER_KV * HEAD_DIM
    return pl.pallas_call(
        functools.partial(_flash_kernel, tq=tq, tk=tk, seq=seq),
        grid=(bsz, N_KV_HEADS, seq // tq),
        in_specs=[pl.BlockSpec((1, tq, gw), lambda b, h, i: (b, i, h)),
                  pl.BlockSpec((1, seq, HEAD_DIM), lambda b, h, i: (b, 0, h)),
                  pl.BlockSpec((1, seq, HEAD_DIM), lambda b, h, i: (b, 0, h))],
        out_specs=pl.BlockSpec((1, tq, gw), lambda b, h, i: (b, i, h)),
        out_shape=jax.ShapeDtypeStruct((bsz, seq, N_HEADS * HEAD_DIM), F32),
        scratch_shapes=[pltpu.VMEM((Q_PER_KV * tq, 1), F32),
                        pltpu.VMEM((Q_PER_KV * tq, 1), F32),
                        pltpu.VMEM((Q_PER_KV * tq, HEAD_DIM), F32)],
        compiler_params=_cparams(3),
        name="flash_attention",
    )(q, k, v)


def _layer_norm(y, g, b):
    mu = jnp.mean(y, axis=-1, keepdims=True)
    yc = y - mu
    var = jnp.mean(yc * yc, axis=-1, keepdims=True)
    return yc * lax.rsqrt(var + NORM_EPS) * g + b


def _merge_kernel(yh_ref, ya_ref, x_ref, gh_ref, ga_ref, wo_ref, lg_ref, lb_ref, wr_ref, br_ref,
                  x1_ref, aff_ref):
    hw = yh_ref.shape[1]
    yh = yh_ref[...]
    ya = ya_ref[...]
    mh = yh * lax.rsqrt(jnp.mean(yh * yh, axis=-1, keepdims=True) + NORM_EPS) * gh_ref[...]
    ma = ya * lax.rsqrt(jnp.mean(ya * ya, axis=-1, keepdims=True) + NORM_EPS) * ga_ref[...]
    y = (jnp.dot(mh.astype(BF16), wo_ref[:hw, :], preferred_element_type=F32)
         + jnp.dot(ma.astype(BF16), wo_ref[hw:, :], preferred_element_type=F32)
         + DN_ALPHA * x_ref[...])
    x1 = _layer_norm(y, lg_ref[...], lb_ref[...])
    x1_ref[...] = x1
    logits = jnp.dot(x1, wr_ref[...], preferred_element_type=F32, precision=HIGHEST) + br_ref[...]
    e = jnp.exp(logits - jnp.max(logits, axis=-1, keepdims=True))
    aff_ref[...] = e / jnp.sum(e, axis=-1, keepdims=True)


def _merge(y_hy, y_att, x2, g_hy, g_attn, wo_bf, ln_g, ln_b, wr_pad, br_pad):
    n, d = x2.shape
    hw = y_hy.shape[1]
    tm = min(256, n)
    row = lambda w: pl.BlockSpec((1, w), lambda i: (0, 0))
    return pl.pallas_call(
        _merge_kernel,
        grid=(n // tm,),
        in_specs=[pl.BlockSpec((tm, hw), lambda i: (i, 0)),
                  pl.BlockSpec((tm, d - hw), lambda i: (i, 0)),
                  pl.BlockSpec((tm, d), lambda i: (i, 0)),
                  row(hw), row(d - hw),
                  pl.BlockSpec((d, d), lambda i: (0, 0)),
                  row(d), row(d),
                  pl.BlockSpec((d, LANES), lambda i: (0, 0)),
                  row(LANES)],
        out_specs=[pl.BlockSpec((tm, d), lambda i: (i, 0)),
                   pl.BlockSpec((tm, LANES), lambda i: (i, 0))],
        out_shape=[jax.ShapeDtypeStruct((n, d), F32), jax.ShapeDtypeStruct((n, LANES), F32)],
        compiler_params=_cparams(1),
        name="merge_ln_router",
    )(y_hy, y_att, x2, g_hy, g_attn, wo_bf, ln_g, ln_b, wr_pad, br_pad)


def _select_kernel(a_ref, idx_ref, pos_ref, gate_ref, excl_ref, *, cap):
    a = a_ref[0]
    ne, nr, _ = a.shape
    bits = lax.bitcast_convert_type(a, I32)

    def count(mask):
        return jnp.sum(jnp.sum(mask.astype(F32), axis=2, keepdims=True), axis=1, keepdims=True)

    thr = jnp.zeros((ne, 1, 1), I32)
    for bit in range(30, -1, -1):
        cand = thr | (1 << bit)
        thr = jnp.where(count(bits >= cand) >= cap, cand, thr)

    ii = lax.broadcasted_iota(I32, (LANES, LANES), 0)
    jj = lax.broadcasted_iota(I32, (LANES, LANES), 1)
    upper = (ii <= jj).astype(BF16)
    ri = lax.broadcasted_iota(I32, (ne, nr, nr), 1)
    rj = lax.broadcasted_iota(I32, (ne, nr, nr), 2)
    lower_strict = (rj < ri).astype(BF16)

    def prefix(mask):
        mb = mask.astype(BF16).reshape(ne * nr, LANES)
        pin = jnp.dot(mb, upper, preferred_element_type=F32).reshape(ne, nr, LANES)
        tot = pin[:, :, LANES - 1:LANES]
        totb = jnp.broadcast_to(tot, (ne, nr, LANES)).astype(BF16)
        off = jnp.einsum("ers,esl->erl", lower_strict, totb, preferred_element_type=F32)
        return pin, tot, off[:, :, 0:1]

    gt = bits > thr
    eq = bits == thr
    need = cap - count(gt)
    pin_eq, _, off_eq = prefix(eq)
    eq_rank = pin_eq + off_eq - eq.astype(F32)
    chosen = gt | (eq & (eq_rank < need))
    pin, tot, off = prefix(chosen)
    chosen_f = chosen.astype(F32)
    excl = pin + off - chosen_f
    excl_ref[0] = excl
    pos_ref[0] = jnp.where(chosen, excl, -1.0)
    gate_ref[0] = jnp.where(chosen, a, 0.0)

    s_row = lax.broadcasted_iota(I32, (nr, cap), 1).astype(F32)
    r_col = lax.broadcasted_iota(I32, (nr, cap), 0).astype(F32)
    for e in range(ne):
        lo = off[e]
        hi = lo + tot[e]
        in_row = ((lo <= s_row) & (s_row < hi)).astype(F32)
        row_of_s = jnp.sum(in_row * r_col, axis=0, keepdims=True)
        s_local = s_row[0:1] - jnp.sum(in_row * lo, axis=0, keepdims=True)
        pin_t = pin[e].T.astype(BF16)
        g = jnp.dot(pin_t, in_row.astype(BF16), preferred_element_type=F32)
        lane_of_s = jnp.sum((g <= s_local).astype(F32), axis=0, keepdims=True)
        idx_ref[0, e:e + 1, :] = (row_of_s * LANES + lane_of_s).astype(I32)


def _select(aff4, cap):
    bsz, ne, nr, _ = aff4.shape
    blk = pl.BlockSpec((1, ne, nr, LANES), lambda b: (b, 0, 0, 0))
    return pl.pallas_call(
        functools.partial(_select_kernel, cap=cap),
        grid=(bsz,),
        in_specs=[blk],
        out_specs=[pl.BlockSpec((1, ne, cap), lambda b: (b, 0, 0)), blk, blk, blk],
        out_shape=[jax.ShapeDtypeStruct((bsz, ne, cap), I32),
                   jax.ShapeDtypeStruct(aff4.shape, F32),
                   jax.ShapeDtypeStruct(aff4.shape, F32),
                   jax.ShapeDtypeStruct(aff4.shape, F32)],
        compiler_params=_cparams(1),
        name="expert_choice_select",
    )(aff4)


def _expert_kernel(idx_ref, x_hbm, wgu_ref, wd_ref, o_ref, xg_ref, sem, *, tc, seq, ff):
    b = pl.program_id(1)
    base = b * seq

    def row_copy(r, tok):
        return pltpu.make_async_copy(x_hbm.at[pl.ds(base + tok, 1)], xg_ref.at[pl.ds(r, 1)], sem)

    def issue(r, carry):
        row_copy(r, idx_ref[0, 0, r]).start()
        return carry

    lax.fori_loop(0, tc, issue, 0)

    def drain(r, carry):
        row_copy(r, 0).wait()
        return carry

    lax.fori_loop(0, tc, drain, 0)
    xb = xg_ref[...].astype(BF16)
    gu = jnp.dot(xb, wgu_ref[0], preferred_element_type=F32)
    gate = gu[:, :ff]
    h = gate * jax.nn.sigmoid(gate) * gu[:, ff:]
    o_ref[0, 0] = jnp.dot(h.astype(BF16), wd_ref[0], preferred_element_type=F32).astype(BF16)


def _experts(idx, x1, wgu_bf, wd_bf, bsz, seq):
    _, ne, cap = idx.shape
    d = x1.shape[1]
    ff = wd_bf.shape[1]
    tc = min(256, cap)
    nj = cap // tc
    idx3 = idx.reshape(bsz * ne * nj, 1, tc)
    return pl.pallas_call(
        functools.partial(_expert_kernel, tc=tc, seq=seq, ff=ff),
        grid=(ne, bsz, nj),
        in_specs=[pl.BlockSpec((1, 1, tc), lambda e, b, j: ((b * ne + e) * nj + j, 0, 0),
                               memory_space=pltpu.SMEM),
                  pl.BlockSpec(memory_space=pl.ANY),
                  pl.BlockSpec((1, d, 2 * ff), lambda e, b, j: (e, 0, 0)),
                  pl.BlockSpec((1, ff, d), lambda e, b, j: (e, 0, 0))],
        out_specs=pl.BlockSpec((1, 1, tc, d), lambda e, b, j: (b, e, j, 0)),
        out_shape=jax.ShapeDtypeStruct((bsz, ne, cap, d), BF16),
        scratch_shapes=[pltpu.VMEM((tc, d), F32), pltpu.SemaphoreType.DMA(())],
        compiler_params=_cparams(3),
        name="expert_ffn",
    )(idx3, x1, wgu_bf, wd_bf)


def _combine_kernel(w0_ref, y_hbm, x1_ref, pos_ref, gate_ref, lg_ref, lb_ref, o_ref, ybuf, sem,
                    *, tb_tokens, win, cap, ne, ntb):
    b = pl.program_id(0)
    tb = pl.program_id(1)
    starts = []
    for e in range(ne):
        s0 = w0_ref[(b * ntb + tb) * ne + e]
        sa = jnp.minimum((s0 // BF16_ROWS) * BF16_ROWS, cap - win)
        sa = pl.multiple_of(sa, BF16_ROWS)
        starts.append(sa)
        pltpu.make_async_copy(y_hbm.at[b, e, pl.ds(sa, win)], ybuf.at[e], sem.at[e]).start()
    acc = DN_ALPHA * x1_ref[...]
    lane = lax.broadcasted_iota(I32, (tb_tokens, win), 1).astype(F32)
    for e in range(ne):
        pltpu.make_async_copy(y_hbm.at[b, e, pl.ds(starts[e], win)], ybuf.at[e], sem.at[e]).wait()
        rel = pos_ref[0, :, e:e + 1] - starts[e].astype(F32)
        onehot = (rel == lane).astype(BF16)
        acc = acc + gate_ref[0, :, e:e + 1] * jnp.dot(onehot, ybuf[e], preferred_element_type=F32)
    o_ref[...] = _layer_norm(acc, lg_ref[...], lb_ref[...])


def _combine(y, x1, pos_t, gate_t, w0, ln_g, ln_b, bsz, seq):
    _, ne, cap, d = y.shape
    tbt = min(256, cap)
    win = min(tbt + BF16_ROWS, cap)
    ntb = seq // tbt
    grid_spec = pltpu.PrefetchScalarGridSpec(
        num_scalar_prefetch=1,
        grid=(bsz, ntb),
        in_specs=[pl.BlockSpec(memory_space=pl.ANY),
                  pl.BlockSpec((tbt, d), lambda b, t, w: (b * ntb + t, 0)),
                  pl.BlockSpec((1, tbt, ne), lambda b, t, w: (b, t, 0)),
                  pl.BlockSpec((1, tbt, ne), lambda b, t, w: (b, t, 0)),
                  pl.BlockSpec((1, d), lambda b, t, w: (0, 0)),
                  pl.BlockSpec((1, d), lambda b, t, w: (0, 0))],
        out_specs=pl.BlockSpec((tbt, d), lambda b, t, w: (b * ntb + t, 0)),
        scratch_shapes=[pltpu.VMEM((ne, win, d), BF16), pltpu.SemaphoreType.DMA((ne,))],
    )
    return pl.pallas_call(
        functools.partial(_combine_kernel, tb_tokens=tbt, win=win, cap=cap, ne=ne, ntb=ntb),
        grid_spec=grid_spec,
        out_shape=jax.ShapeDtypeStruct((bsz * seq, d), F32),
        compiler_params=_cparams(2),
        name="moe_combine_ln",
    )(w0, y, x1, pos_t, gate_t, ln_g, ln_b)


def _deinterleave(width):
    idx = np.arange(width).reshape(-1, HEAD_DIM // 2, 2)
    return np.concatenate([idx[:, :, 0], idx[:, :, 1]], axis=1).reshape(-1)


def kernel(x, w_in, b_in, hy_conv_w, hy_conv_b, hy_ffn_w1, hy_ffn_b1, hy_sin_f1, hy_ffn_w2, hy_ffn_b2, hy_sin_f2, hy_ffn_w3, hy_decay, hy_skip, q_norm, k_norm, g_hy, g_attn, w_out, ln1_g, ln1_b, w_router, b_router, w_gate, w_up, w_down, ln2_g, ln2_b):
    bsz, seq, d = x.shape
    n = bsz * seq
    x2 = x.reshape(n, d)
    s1 = 3 * HY_WIDTH
    qw = N_HEADS * HEAD_DIM
    kw = N_KV_HEADS * HEAD_DIM
    perm = np.concatenate([np.arange(s1), s1 + _deinterleave(qw), s1 + qw + _deinterleave(kw),
                           np.arange(s1 + qw + kw, w_in.shape[1])])
    w_in_bf = w_in[:, perm].astype(BF16)
    b_in_p = b_in[perm].reshape(1, -1)
    hd_perm = _deinterleave(HEAD_DIM)

    proj = _in_projection(x2, w_in_bf, b_in_p)

    hv, hx1, hx2 = _short_conv(proj, hy_conv_w, hy_conv_b.reshape(1, -1), seq)
    filt = _hyena_filters(seq, hy_ffn_w1, hy_ffn_b1, hy_sin_f1, hy_ffn_w2, hy_ffn_b2, hy_sin_f2,
                          hy_ffn_w3, hy_decay)
    y_hy = _hyena_mix(hv, hx1, hx2, filt, hy_skip, bsz, seq)

    qb, kb, vb = _qk_prep(proj, q_norm[hd_perm].reshape(1, HEAD_DIM), k_norm[hd_perm].reshape(1, HEAD_DIM), seq)
    y_att = _flash_attention(qb.reshape(bsz, seq, qw), kb.reshape(bsz, seq, kw), vb.reshape(bsz, seq, kw),
                             bsz, seq)

    ne = w_router.shape[1]
    wr_pad = jnp.zeros((d, LANES), F32).at[:, :ne].set(w_router)
    br_pad = jnp.full((1, LANES), -1e30, F32).at[0, :ne].set(b_router)
    x1, aff = _merge(y_hy, y_att.reshape(n, qw), x2, g_hy.reshape(1, -1), g_attn.reshape(1, -1),
                     w_out.astype(BF16), ln1_g.reshape(1, d), ln1_b.reshape(1, d), wr_pad, br_pad)

    cap = CAPACITY_FACTOR * seq // ne
    aff4 = aff[:, :ne].reshape(bsz, seq, ne).transpose(0, 2, 1).reshape(bsz, ne, seq // LANES, LANES)
    idx, pos, gate, excl = _select(aff4, cap)
    wgu_bf = jnp.concatenate([w_gate, w_up], axis=2).astype(BF16)
    y = _experts(idx, x1, wgu_bf, w_down.astype(BF16), bsz, seq)
    pos_t = pos.reshape(bsz, ne, seq).transpose(0, 2, 1)
    gate_t = gate.reshape(bsz, ne, seq).transpose(0, 2, 1)
    tbt = min(256, cap)
    w0 = excl.reshape(bsz, ne, seq // tbt, tbt)[:, :, :, 0].transpose(0, 2, 1).reshape(-1).astype(I32)
    out = _combine(y, x1, pos_t, gate_t, w0, ln2_g.reshape(1, d), ln2_b.reshape(1, d), bsz, seq)
    return out.reshape(bsz, seq, d)
```

```python
import functools
import math

import numpy as np
import jax
import jax.numpy as jnp
from jax import lax
from jax.experimental import pallas as pl
from jax.experimental.pallas import tpu as pltpu

F32 = jnp.float32
BF16 = jnp.bfloat16
I32 = jnp.int32
HIGHEST = lax.Precision.HIGHEST

HY_WIDTH = 1024
HY_BANDS = 16
HEAD_DIM = 128
N_HEADS = 8
N_KV_HEADS = 2
Q_PER_KV = N_HEADS // N_KV_HEADS
ROPE_THETA = 10000.0
GRID_W = 64
N_EXPERTS = 16
CAPACITY_FACTOR = 2
NORM_EPS = 1e-6
DN_ALPHA = 2.0 ** 0.25
LOG2_E = math.log2(math.e)

LANES = 128
SUBLANES = 8
BF16_ROWS = 16
FFT_N1 = 128
VMEM_LIMIT = 56 * 1024 * 1024


def _cparams(n_axes, vmem=VMEM_LIMIT):
    return pltpu.CompilerParams(dimension_semantics=("arbitrary",) * n_axes, vmem_limit_bytes=vmem)


def _round_up(a, m):
    return (a + m - 1) // m * m


def _inproj_kernel(x_ref, w_ref, b_ref, o_ref, xb_ref):
    @pl.when(pl.program_id(1) == 0)
    def _():
        xb_ref[...] = x_ref[...].astype(BF16)

    o_ref[...] = jnp.dot(xb_ref[...], w_ref[...], preferred_element_type=F32) + b_ref[...]


def _in_projection(x2, w_bf, b_row):
    n, d = x2.shape
    wdt = w_bf.shape[1]
    tm = min(1024, n)
    tn = 512
    return pl.pallas_call(
        _inproj_kernel,
        grid=(n // tm, wdt // tn),
        in_specs=[pl.BlockSpec((tm, d), lambda i, j: (i, 0)),
                  pl.BlockSpec((d, tn), lambda i, j: (0, j)),
                  pl.BlockSpec((1, tn), lambda i, j: (0, j))],
        out_specs=pl.BlockSpec((tm, tn), lambda i, j: (i, j)),
        out_shape=jax.ShapeDtypeStruct((n, wdt), F32),
        scratch_shapes=[pltpu.VMEM((tm, d), BF16)],
        compiler_params=_cparams(2),
        name="in_projection",
    )(x2, w_bf, b_row)


def _shortconv_kernel(*refs, tt, seq):
    mains, prevs, nexts = refs[0:3], refs[3:6], refs[6:9]
    w_refs, b_refs, outs = refs[9:12], refs[12:15], refs[15:18]
    i = pl.program_id(0)
    r_in = lax.broadcasted_iota(I32, (tt, 1), 0)
    tpos = (i * tt + r_in) % seq
    for m, p, nx, w, b, o in zip(mains, prevs, nexts, w_refs, b_refs, outs):
        cur = m[...]
        up = pltpu.roll(cur, 1, axis=0)
        up = jnp.where(r_in == 0, p[SUBLANES - 1:SUBLANES, :], up)
        up = jnp.where(tpos == 0, 0.0, up)
        dn = pltpu.roll(cur, tt - 1, axis=0)
        dn = jnp.where(r_in == tt - 1, nx[0:1, :], dn)
        dn = jnp.where(tpos == seq - 1, 0.0, dn)
        wv = w[...]
        o[...] = wv[0:1] * up + wv[1:2] * cur + wv[2:3] * dn + b[...]


def _short_conv(proj, conv_w, conv_b, seq):
    n = proj.shape[0]
    tt = min(512, seq)
    tc = 512
    ncb = HY_WIDTH // tc
    nrb = n // SUBLANES
    hb = tt // SUBLANES
    main = [pl.BlockSpec((tt, tc), functools.partial(lambda i, j, p: (i, p * ncb + j), p=p)) for p in range(3)]
    prev = [pl.BlockSpec((SUBLANES, tc),
                         functools.partial(lambda i, j, p: (jnp.maximum(i * hb - 1, 0), p * ncb + j), p=p))
            for p in range(3)]
    nxt = [pl.BlockSpec((SUBLANES, tc),
                        functools.partial(lambda i, j, p: (jnp.minimum((i + 1) * hb, nrb - 1), p * ncb + j), p=p))
           for p in range(3)]
    wsp = [pl.BlockSpec((3, tc), functools.partial(lambda i, j, p: (0, p * ncb + j), p=p)) for p in range(3)]
    bsp = [pl.BlockSpec((1, tc), functools.partial(lambda i, j, p: (0, p * ncb + j), p=p)) for p in range(3)]
    osp = [pl.BlockSpec((tt, tc), lambda i, j: (i, j)) for _ in range(3)]
    return pl.pallas_call(
        functools.partial(_shortconv_kernel, tt=tt, seq=seq),
        grid=(n // tt, ncb),
        in_specs=main + prev + nxt + wsp + bsp,
        out_specs=osp,
        out_shape=[jax.ShapeDtypeStruct((n, HY_WIDTH), F32)] * 3,
        compiler_params=_cparams(2),
        name="short_conv",
    )(proj, proj, proj, proj, proj, proj, proj, proj, proj,
      conv_w, conv_w, conv_w, conv_b, conv_b, conv_b)


def _filter_kernel(freq_ref, w1_ref, b1_ref, f1_ref, w2_ref, b2_ref, f2_ref, w3_ref, dec_ref, o_ref, *, tt, seq):
    i = pl.program_id(0)
    r = (i * tt + lax.broadcasted_iota(I32, (tt, LANES), 0)).astype(F32)
    lane = lax.broadcasted_iota(I32, (tt, LANES), 1)
    t = r / float(seq - 1)
    w = (2.0 * math.pi / seq) * r
    arg = freq_ref[...] * w
    emb = jnp.where(lane == 0, t,
                    jnp.where(lane <= HY_BANDS, jnp.cos(arg),
                              jnp.where(lane <= 2 * HY_BANDS, -jnp.sin(arg), 0.0)))
    h = jnp.sin(f1_ref[...] * (jnp.dot(emb, w1_ref[...], preferred_element_type=F32, precision=HIGHEST)
                               + b1_ref[...]))
    h = jnp.sin(f2_ref[...] * (jnp.dot(h, w2_ref[...], preferred_element_type=F32, precision=HIGHEST)
                               + b2_ref[...]))
    h3 = jnp.dot(h, w3_ref[...], preferred_element_type=F32, precision=HIGHEST)
    win = jnp.exp(-t[:, 0:1] * jnp.abs(dec_ref[...]))
    out = h3 * win
    col = lax.broadcasted_iota(I32, out.shape, 1)
    row = i * tt + lax.broadcasted_iota(I32, out.shape, 0)
    o_ref[...] = jnp.where((row == 0) & (col >= out.shape[1] // 2), 0.0, out)


def _hyena_filters(seq, w1, b1, f1, w2, b2, f2, w3, decay):
    ffn = w1.shape[1]
    nout = w3.shape[1]
    tt = min(256, seq)
    band = jnp.linspace(1e-4, HY_BANDS - 1, HY_BANDS, dtype=F32)
    freq = jnp.zeros((1, LANES), F32).at[0, 1:1 + HY_BANDS].set(band).at[0, 1 + HY_BANDS:1 + 2 * HY_BANDS].set(band)
    w1p = jnp.zeros((LANES, ffn), F32).at[:w1.shape[0]].set(w1)
    full = lambda shape: pl.BlockSpec(shape, lambda i: (0,) * len(shape))
    return pl.pallas_call(
        functools.partial(_filter_kernel, tt=tt, seq=seq),
        grid=(seq // tt,),
        in_specs=[full((1, LANES)), full((LANES, ffn)), full((1, ffn)), full((1, ffn)),
                  full((ffn, ffn)), full((1, ffn)), full((1, ffn)), full((ffn, nout)), full((1, nout))],
        out_specs=pl.BlockSpec((tt, nout), lambda i: (i, 0)),
        out_shape=jax.ShapeDtypeStruct((seq, nout), F32),
        compiler_params=_cparams(1),
        name="hyena_filters",
    )(freq, w1p, b1.reshape(1, ffn), f1.reshape(1, ffn), w2, b2.reshape(1, ffn), f2.reshape(1, ffn),
      w3, decay.reshape(1, nout))


def _fft_tables(seq):
    n2h = seq // FFT_N1
    n2 = 2 * n2h
    n = FFT_N1 * n2
    k2n = n2h + 1
    k2p = _round_up(k2n, BF16_ROWS)
    kk = np.arange(k2p)[:, None]
    nn = np.arange(n2h)[None, :]
    ang = 2.0 * np.pi * ((kk * nn) % n2) / n2
    valid = (kk < k2n).astype(np.float64)
    fa = np.concatenate([np.cos(ang) * valid, -np.sin(ang) * valid], axis=0)
    wk = np.where((kk == 0) | (kk == n2h), 1.0, 2.0) * valid
    gc = np.concatenate([(np.cos(ang) * wk).T, (-np.sin(ang) * wk).T], axis=1)
    n1 = np.arange(FFT_N1)[None, :]
    tw_ang = 2.0 * np.pi * ((kk * n1) % n) / n
    twr = (np.cos(tw_ang) * valid).reshape(k2p, 1, FFT_N1)
    twi = (-np.sin(tw_ang) * valid).reshape(k2p, 1, FFT_N1)
    k1 = np.arange(FFT_N1)[:, None]
    base = 2.0 * np.pi * ((k1 * n1) % FFT_N1) / FFT_N1
    return dict(n2h=n2h, k2p=k2p, n=n,
                fa=jnp.asarray(fa, BF16), gc=jnp.asarray(gc, BF16),
                twr=jnp.asarray(twr, F32), twi=jnp.asarray(twi, F32),
                cr=jnp.asarray(np.cos(base), F32), ci=jnp.asarray(-np.sin(base), F32))


def _ffta_kernel(x_ref, fa_ref, o_ref):
    xb = x_ref[0].astype(BF16)
    o_ref[0] = jnp.dot(fa_ref[...], xb, preferred_element_type=F32).astype(BF16)


def _fft_stage_a(xv, fa):
    bx, n2h, cols = xv.shape
    rows = fa.shape[0]
    cb = min(2048, cols)
    return pl.pallas_call(
        _ffta_kernel,
        grid=(bx, cols // cb),
        in_specs=[pl.BlockSpec((1, n2h, cb), lambda b, j: (b, 0, j)),
                  pl.BlockSpec((rows, n2h), lambda b, j: (0, 0))],
        out_specs=pl.BlockSpec((1, rows, cb), lambda b, j: (b, 0, j)),
        out_shape=jax.ShapeDtypeStruct((bx, rows, cols), BF16),
        compiler_params=_cparams(2),
        name="fft_stage_a",
    )(xv, fa)


def _stage_b_matrix(twr_ref, twi_ref, cr_ref, ci_ref):
    tr = twr_ref[0]
    ti = twi_ref[0]
    cr = cr_ref[...]
    ci = ci_ref[...]
    mr = cr * tr - ci * ti
    mi = ci * tr + cr * ti
    return mr, mi


def _fftb_filter_kernel(a_ref, twr_ref, twi_ref, cr_ref, ci_ref, o_ref, *, scale):
    mr, mi = _stage_b_matrix(twr_ref, twi_ref, cr_ref, ci_ref)
    fb = jnp.concatenate([jnp.concatenate([mr, -mi], axis=1),
                          jnp.concatenate([mi, mr], axis=1)], axis=0).astype(BF16)
    a = jnp.concatenate([a_ref[0, 0, 0], a_ref[0, 1, 0]], axis=0)
    h = jnp.dot(fb, a, preferred_element_type=F32)
    half = h.shape[1] // 2
    hr, hi = h[:FFT_N1], h[FFT_N1:]
    o_ref[0, 0] = (hr[:, :half] + hr[:, half:]) * scale
    o_ref[0, 1] = (hi[:, :half] - hi[:, half:]) * scale


def _fft_stage_b_filter(a5, tabs, scale):
    k2p = tabs["k2p"]
    c4 = a5.shape[-1]
    return pl.pallas_call(
        functools.partial(_fftb_filter_kernel, scale=scale),
        grid=(k2p,),
        in_specs=[pl.BlockSpec((1, 2, 1, FFT_N1, c4), lambda k: (0, 0, k, 0, 0)),
                  pl.BlockSpec((1, 1, FFT_N1), lambda k: (k, 0, 0)),
                  pl.BlockSpec((1, 1, FFT_N1), lambda k: (k, 0, 0)),
                  pl.BlockSpec((FFT_N1, FFT_N1), lambda k: (0, 0)),
                  pl.BlockSpec((FFT_N1, FFT_N1), lambda k: (0, 0))],
        out_specs=pl.BlockSpec((1, 2, FFT_N1, c4 // 2), lambda k: (k, 0, 0, 0)),
        out_shape=jax.ShapeDtypeStruct((k2p, 2, FFT_N1, c4 // 2), F32),
        compiler_params=_cparams(1),
        name="fft_stage_b_filter",
    )(a5, tabs["twr"], tabs["twi"], tabs["cr"], tabs["ci"])


def _fftb_conv_kernel(a_ref, g_ref, twr_ref, twi_ref, cr_ref, ci_ref, o_ref):
    mr, mi = _stage_b_matrix(twr_ref, twi_ref, cr_ref, ci_ref)
    fb = jnp.concatenate([jnp.concatenate([mr, -mi], axis=1),
                          jnp.concatenate([mi, mr], axis=1)], axis=0).astype(BF16)
    mrt, mit = mr.T, mi.T
    fbt = jnp.concatenate([jnp.concatenate([mrt, mit], axis=1),
                           jnp.concatenate([-mit, mrt], axis=1)], axis=0).astype(BF16)
    gr = g_ref[0, 0]
    gi = g_ref[0, 1]
    for b in range(a_ref.shape[0]):
        a = jnp.concatenate([a_ref[b, 0, 0], a_ref[b, 1, 0]], axis=0)
        xh = jnp.dot(fb, a, preferred_element_type=F32)
        xr, xi = xh[:FFT_N1], xh[FFT_N1:]
        y = jnp.concatenate([xr * gr - xi * gi, xr * gi + xi * gr], axis=0).astype(BF16)
        c = jnp.dot(fbt, y, preferred_element_type=F32)
        o_ref[b, 0, 0] = c[:FFT_N1].astype(BF16)
        o_ref[b, 1, 0] = c[FFT_N1:].astype(BF16)


def _fft_stage_b_conv(a5, g, order, tabs):
    bsz, _, k2p, _, c = a5.shape
    return pl.pallas_call(
        _fftb_conv_kernel,
        grid=(k2p,),
        in_specs=[pl.BlockSpec((bsz, 2, 1, FFT_N1, c), lambda k: (0, 0, k, 0, 0)),
                  pl.BlockSpec((1, 2, FFT_N1, c), lambda k: (k, 0, 0, order)),
                  pl.BlockSpec((1, 1, FFT_N1), lambda k: (k, 0, 0)),
                  pl.BlockSpec((1, 1, FFT_N1), lambda k: (k, 0, 0)),
                  pl.BlockSpec((FFT_N1, FFT_N1), lambda k: (0, 0)),
                  pl.BlockSpec((FFT_N1, FFT_N1), lambda k: (0, 0))],
        out_specs=pl.BlockSpec((bsz, 2, 1, FFT_N1, c), lambda k: (0, 0, k, 0, 0)),
        out_shape=jax.ShapeDtypeStruct(a5.shape, BF16),
        compiler_params=_cparams(1),
        name="fft_stage_b_conv",
    )(a5, g, tabs["twr"], tabs["twi"], tabs["cr"], tabs["ci"])


def _fftc_kernel(c_ref, gc_ref, u_ref, gate_ref, d_ref, o_ref):
    y = jnp.dot(gc_ref[...], c_ref[0], preferred_element_type=F32)
    o_ref[0] = gate_ref[0] * (y + d_ref[...] * u_ref[0])


def _fft_stage_c(cm, gc, uv, gatev, d_blk):
    bsz, rows, cols = cm.shape
    n2h = uv.shape[1]
    cb = d_blk.shape[1]
    return pl.pallas_call(
        _fftc_kernel,
        grid=(bsz, cols // cb),
        in_specs=[pl.BlockSpec((1, rows, cb), lambda b, j: (b, 0, j)),
                  pl.BlockSpec((n2h, rows), lambda b, j: (0, 0)),
                  pl.BlockSpec((1, n2h, cb), lambda b, j: (b, 0, j)),
                  pl.BlockSpec((1, n2h, cb), lambda b, j: (b, 0, j)),
                  pl.BlockSpec((1, cb), lambda b, j: (0, 0))],
        out_specs=pl.BlockSpec((1, n2h, cb), lambda b, j: (b, 0, j)),
        out_shape=jax.ShapeDtypeStruct(uv.shape, F32),
        compiler_params=_cparams(2),
        name="fft_stage_c",
    )(cm, gc, uv, gatev, d_blk)


def _hyena_mix(hv, hx1, hx2, filt, skip, bsz, seq):
    c = hv.shape[1]
    tabs = _fft_tables(seq)
    n2h, k2p, n = tabs["n2h"], tabs["k2p"], tabs["n"]
    cols = FFT_N1 * c
    cb = min(2048, cols)
    fa_f = _fft_stage_a(filt.reshape(1, n2h, FFT_N1 * 4 * c), tabs["fa"])
    g = _fft_stage_b_filter(fa_f.reshape(1, 2, k2p, FFT_N1, 4 * c), tabs, 1.0 / n)

    def conv(u, gate, order):
        a = _fft_stage_a(u.reshape(bsz, n2h, cols), tabs["fa"])
        cm = _fft_stage_b_conv(a.reshape(bsz, 2, k2p, FFT_N1, c), g, order, tabs)
        d_blk = jnp.tile(skip[order].reshape(1, c), (1, cb // c))
        z = _fft_stage_c(cm.reshape(bsz, 2 * k2p, cols), tabs["gc"], u.reshape(bsz, n2h, cols),
                         gate.reshape(bsz, n2h, cols), d_blk)
        return z.reshape(bsz * seq, c)

    z = conv(hv, hx1, 0)
    return conv(z, hx2, 1)


def _qkprep_kernel(q_ref, k_ref, v_ref, qn_ref, kn_ref, oqt_ref, ok_ref, ovt_ref, *, tt, seq):
    i = pl.program_id(0)
    t = (i * tt + lax.broadcasted_iota(I32, (tt, HEAD_DIM), 0)) % seq
    lane = lax.broadcasted_iota(I32, (tt, HEAD_DIM), 1)
    j = lane % (HEAD_DIM // 2)
    quarter = HEAD_DIM // 4
    f = (j % quarter).astype(F32)
    inv = jnp.exp(f * (-2.0 * math.log(ROPE_THETA) / (HEAD_DIM // 2)))
    pos = jnp.where(j < quarter, t // GRID_W, t % GRID_W).astype(F32)
    ang = pos * inv
    cosv = jnp.cos(ang)
    sinv = jnp.sin(ang)
    sin_signed = jnp.where(lane < HEAD_DIM // 2, -sinv, sinv)

    def norm_rope(x, g, scale):
        ms = jnp.mean(x * x, axis=-1, keepdims=True)
        xn = x * lax.rsqrt(ms + NORM_EPS) * g
        return (xn * cosv + pltpu.roll(xn, HEAD_DIM // 2, axis=1) * sin_signed) * scale

    qg = qn_ref[...]
    kg = kn_ref[...]
    q_scale = LOG2_E * HEAD_DIM ** -0.5
    for h in range(N_HEADS):
        sl = slice(h * HEAD_DIM, (h + 1) * HEAD_DIM)
        oqt_ref[h] = norm_rope(q_ref[:, sl], qg, q_scale).T.astype(BF16)
    for h in range(N_KV_HEADS):
        sl = slice(h * HEAD_DIM, (h + 1) * HEAD_DIM)
        ok_ref[:, sl] = norm_rope(k_ref[:, sl], kg, 1.0).astype(BF16)
        ovt_ref[h] = v_ref[:, sl].T.astype(BF16)


def _qk_prep(proj, qn, kn, seq):
    n = proj.shape[0]
    tt = min(512, seq)
    qw = N_HEADS * HEAD_DIM
    kw = N_KV_HEADS * HEAD_DIM
    q_off = (3 * HY_WIDTH) // qw
    k_off = (3 * HY_WIDTH + qw) // kw
    return pl.pallas_call(
        functools.partial(_qkprep_kernel, tt=tt, seq=seq),
        grid=(n // tt,),
        in_specs=[pl.BlockSpec((tt, qw), lambda i: (i, q_off)),
                  pl.BlockSpec((tt, kw), lambda i: (i, k_off)),
                  pl.BlockSpec((tt, kw), lambda i: (i, k_off + 1)),
                  pl.BlockSpec((1, HEAD_DIM), lambda i: (0, 0)),
                  pl.BlockSpec((1, HEAD_DIM), lambda i: (0, 0))],
        out_specs=[pl.BlockSpec((N_HEADS, HEAD_DIM, tt), lambda i: (0, 0, i)),
                   pl.BlockSpec((tt, kw), lambda i: (i, 0)),
                   pl.BlockSpec((N_KV_HEADS, HEAD_DIM, tt), lambda i: (0, 0, i))],
        out_shape=[jax.ShapeDtypeStruct((N_HEADS, HEAD_DIM, n), BF16),
                   jax.ShapeDtypeStruct((n, kw), BF16),
                   jax.ShapeDtypeStruct((N_KV_HEADS, HEAD_DIM, n), BF16)],
        compiler_params=_cparams(1),
        name="qk_prep",
    )(proj, proj, proj, qn, kn)


def _flash_kernel(qt_ref, k_ref, vt_ref, o_ref, acc_sc, *, tq, tk, seq, unroll):
    nq = Q_PER_KV * tq
    qt = jnp.concatenate([qt_ref[g] for g in range(Q_PER_KV)], axis=1)
    acc_sc[...] = jnp.zeros(acc_sc.shape, F32)

    def chunk(start):
        kc = k_ref[pl.ds(start, tk), :]
        vc = vt_ref[0, :, pl.ds(start, tk)]
        s = jnp.dot(kc, qt, preferred_element_type=F32)
        mc = jnp.max(s, axis=0, keepdims=True)
        p = jnp.exp2(s - mc)
        lc = jnp.sum(p, axis=0, keepdims=True)
        oc = jnp.dot(vc, p.astype(BF16), preferred_element_type=F32)
        return mc, lc, oc

    def body(i, carry):
        m, l = carry
        for u in range(unroll):
            start = pl.multiple_of((i * unroll + u) * tk, tk)
            mc, lc, oc = chunk(start)
            m_new = jnp.maximum(m, mc)
            a = jnp.exp2(m - m_new)
            bc = jnp.exp2(mc - m_new)
            l = l * a + lc * bc
            acc_sc[...] = acc_sc[...] * a + oc * bc
            m = m_new
        return m, l

    m0 = jnp.full((1, nq), -jnp.inf, F32)
    l0 = jnp.zeros((1, nq), F32)
    _, l = lax.fori_loop(0, seq // (tk * unroll), body, (m0, l0))
    o = acc_sc[...] / l
    for g in range(Q_PER_KV):
        o_ref[:, g * HEAD_DIM:(g + 1) * HEAD_DIM] = o[:, g * tq:(g + 1) * tq].T


def _flash_attention(qt, k, vt, bsz, seq):
    n = bsz * seq
    tq = min(256, seq)
    tk = min(512, seq)
    unroll = 2 if seq // tk >= 2 else 1
    gw = Q_PER_KV * HEAD_DIM
    nqb = seq // tq
    return pl.pallas_call(
        functools.partial(_flash_kernel, tq=tq, tk=tk, seq=seq, unroll=unroll),
        grid=(bsz, N_KV_HEADS, nqb),
        in_specs=[pl.BlockSpec((Q_PER_KV, HEAD_DIM, tq), lambda b, h, i: (h, 0, b * nqb + i)),
                  pl.BlockSpec((seq, HEAD_DIM), lambda b, h, i: (b, h)),
                  pl.BlockSpec((1, HEAD_DIM, seq), lambda b, h, i: (h, 0, b))],
        out_specs=pl.BlockSpec((tq, gw), lambda b, h, i: (b * nqb + i, h)),
        out_shape=jax.ShapeDtypeStruct((n, N_HEADS * HEAD_DIM), F32),
        scratch_shapes=[pltpu.VMEM((HEAD_DIM, Q_PER_KV * tq), F32)],
        compiler_params=_cparams(3),
        name="flash_attention",
    )(qt, k, vt)


def _layer_norm(y, g, b):
    mu = jnp.mean(y, axis=-1, keepdims=True)
    yc = y - mu
    var = jnp.mean(yc * yc, axis=-1, keepdims=True)
    return yc * lax.rsqrt(var + NORM_EPS) * g + b


def _merge_kernel(yh_ref, ya_ref, x_ref, gh_ref, ga_ref, wo_ref, lg_ref, lb_ref, wr_ref, br_ref,
                  x1_ref, aff_ref):
    hw = yh_ref.shape[1]
    yh = yh_ref[...]
    ya = ya_ref[...]
    mh = yh * lax.rsqrt(jnp.mean(yh * yh, axis=-1, keepdims=True) + NORM_EPS) * gh_ref[...]
    ma = ya * lax.rsqrt(jnp.mean(ya * ya, axis=-1, keepdims=True) + NORM_EPS) * ga_ref[...]
    y = (jnp.dot(mh.astype(BF16), wo_ref[:hw, :], preferred_element_type=F32)
         + jnp.dot(ma.astype(BF16), wo_ref[hw:, :], preferred_element_type=F32)
         + DN_ALPHA * x_ref[...])
    x1 = _layer_norm(y, lg_ref[...], lb_ref[...])
    x1_ref[...] = x1
    logits = jnp.dot(x1, wr_ref[...], preferred_element_type=F32, precision=HIGHEST) + br_ref[...]
    e = jnp.exp(logits - jnp.max(logits, axis=-1, keepdims=True))
    aff_ref[...] = e / jnp.sum(e, axis=-1, keepdims=True)


def _merge(y_hy, y_att, x2, g_hy, g_attn, wo_bf, ln_g, ln_b, wr_pad, br_pad):
    n, d = x2.shape
    hw = y_hy.shape[1]
    tm = min(256, n)
    row = lambda w: pl.BlockSpec((1, w), lambda i: (0, 0))
    return pl.pallas_call(
        _merge_kernel,
        grid=(n // tm,),
        in_specs=[pl.BlockSpec((tm, hw), lambda i: (i, 0)),
                  pl.BlockSpec((tm, d - hw), lambda i: (i, 0)),
                  pl.BlockSpec((tm, d), lambda i: (i, 0)),
                  row(hw), row(d - hw),
                  pl.BlockSpec((d, d), lambda i: (0, 0)),
                  row(d), row(d),
                  pl.BlockSpec((d, LANES), lambda i: (0, 0)),
                  row(LANES)],
        out_specs=[pl.BlockSpec((tm, d), lambda i: (i, 0)),
                   pl.BlockSpec((tm, LANES), lambda i: (i, 0))],
        out_shape=[jax.ShapeDtypeStruct((n, d), F32), jax.ShapeDtypeStruct((n, LANES), F32)],
        compiler_params=_cparams(1),
        name="merge_ln_router",
    )(y_hy, y_att, x2, g_hy, g_attn, wo_bf, ln_g, ln_b, wr_pad, br_pad)


def _select_kernel(a_ref, idx_ref, pos_ref, gate_ref, excl_ref, *, cap):
    a = a_ref[0]
    ne, nr, _ = a.shape
    bits = lax.bitcast_convert_type(a, I32)

    def count(mask):
        return jnp.sum(jnp.sum(mask.astype(F32), axis=2, keepdims=True), axis=1, keepdims=True)

    thr = jnp.zeros((ne, 1, 1), I32)
    for bit in range(30, -1, -1):
        cand = thr | (1 << bit)
        thr = jnp.where(count(bits >= cand) >= cap, cand, thr)

    ii = lax.broadcasted_iota(I32, (LANES, LANES), 0)
    jj = lax.broadcasted_iota(I32, (LANES, LANES), 1)
    upper = (ii <= jj).astype(BF16)
    ri = lax.broadcasted_iota(I32, (ne, nr, nr), 1)
    rj = lax.broadcasted_iota(I32, (ne, nr, nr), 2)
    lower_strict = (rj < ri).astype(BF16)

    def prefix(mask):
        mb = mask.astype(BF16).reshape(ne * nr, LANES)
        pin = jnp.dot(mb, upper, preferred_element_type=F32).reshape(ne, nr, LANES)
        tot = pin[:, :, LANES - 1:LANES]
        totb = jnp.broadcast_to(tot, (ne, nr, LANES)).astype(BF16)
        off = jnp.einsum("ers,esl->erl", lower_strict, totb, preferred_element_type=F32)
        return pin, tot, off[:, :, 0:1]

    gt = bits > thr
    eq = bits == thr
    need = cap - count(gt)
    pin_eq, _, off_eq = prefix(eq)
    eq_rank = pin_eq + off_eq - eq.astype(F32)
    chosen = gt | (eq & (eq_rank < need))
    pin, tot, off = prefix(chosen)
    chosen_f = chosen.astype(F32)
    excl = pin + off - chosen_f
    excl_ref[0] = excl
    pos_ref[0] = jnp.where(chosen, excl, -1.0)
    gate_ref[0] = jnp.where(chosen, a, 0.0)

    s_row = lax.broadcasted_iota(I32, (nr, cap), 1).astype(F32)
    r_col = lax.broadcasted_iota(I32, (nr, cap), 0).astype(F32)
    for e in range(ne):
        lo = off[e]
        hi = lo + tot[e]
        in_row = ((lo <= s_row) & (s_row < hi)).astype(F32)
        row_of_s = jnp.sum(in_row * r_col, axis=0, keepdims=True)
        s_local = s_row[0:1] - jnp.sum(in_row * lo, axis=0, keepdims=True)
        pin_t = pin[e].T.astype(BF16)
        g = jnp.dot(pin_t, in_row.astype(BF16), preferred_element_type=F32)
        lane_of_s = jnp.sum((g <= s_local).astype(F32), axis=0, keepdims=True)
        idx_ref[0, e:e + 1, :] = (row_of_s * LANES + lane_of_s).astype(I32)


def _select(aff4, cap):
    bsz, ne, nr, _ = aff4.shape
    blk = pl.BlockSpec((1, ne, nr, LANES), lambda b: (b, 0, 0, 0))
    return pl.pallas_call(
        functools.partial(_select_kernel, cap=cap),
        grid=(bsz,),
        in_specs=[blk],
        out_specs=[pl.BlockSpec((1, ne, cap), lambda b: (b, 0, 0)), blk, blk, blk],
        out_shape=[jax.ShapeDtypeStruct((bsz, ne, cap), I32),
                   jax.ShapeDtypeStruct(aff4.shape, F32),
                   jax.ShapeDtypeStruct(aff4.shape, F32),
                   jax.ShapeDtypeStruct(aff4.shape, F32)],
        compiler_params=_cparams(1),
        name="expert_choice_select",
    )(aff4)


def _expert_kernel(idx_ref, x_hbm, wgu_ref, wd_ref, o_ref, xg_ref, sem, *, tc, seq, ff):
    b = pl.program_id(1)
    base = b * seq

    def row_copy(r, tok):
        return pltpu.make_async_copy(x_hbm.at[pl.ds(base + tok, 1)], xg_ref.at[pl.ds(r, 1)], sem)

    def issue(r, carry):
        row_copy(r, idx_ref[0, 0, r]).start()
        return carry

    lax.fori_loop(0, tc, issue, 0)

    def drain(r, carry):
        row_copy(r, 0).wait()
        return carry

    lax.fori_loop(0, tc, drain, 0)
    xb = xg_ref[...].astype(BF16)
    gu = jnp.dot(xb, wgu_ref[0], preferred_element_type=F32)
    gate = gu[:, :ff]
    h = gate * jax.nn.sigmoid(gate) * gu[:, ff:]
    o_ref[0, 0] = jnp.dot(h.astype(BF16), wd_ref[0], preferred_element_type=F32).astype(BF16)


def _experts(idx, x1, wgu_bf, wd_bf, bsz, seq):
    _, ne, cap = idx.shape
    d = x1.shape[1]
    ff = wd_bf.shape[1]
    tc = min(256, cap)
    nj = cap // tc
    idx3 = idx.reshape(bsz * ne * nj, 1, tc)
    return pl.pallas_call(
        functools.partial(_expert_kernel, tc=tc, seq=seq, ff=ff),
        grid=(ne, bsz, nj),
        in_specs=[pl.BlockSpec((1, 1, tc), lambda e, b, j: ((b * ne + e) * nj + j, 0, 0),
                               memory_space=pltpu.SMEM),
                  pl.BlockSpec(memory_space=pl.ANY),
                  pl.BlockSpec((1, d, 2 * ff), lambda e, b, j: (e, 0, 0)),
                  pl.BlockSpec((1, ff, d), lambda e, b, j: (e, 0, 0))],
        out_specs=pl.BlockSpec((1, 1, tc, d), lambda e, b, j: (b, e, j, 0)),
        out_shape=jax.ShapeDtypeStruct((bsz, ne, cap, d), BF16),
        scratch_shapes=[pltpu.VMEM((tc, d), F32), pltpu.SemaphoreType.DMA(())],
        compiler_params=_cparams(3),
        name="expert_ffn",
    )(idx3, x1, wgu_bf, wd_bf)


def _combine_kernel(w0_ref, y_hbm, x1_ref, pos_ref, gate_ref, lg_ref, lb_ref, o_ref, ybuf, sem,
                    *, tb_tokens, win, cap, ne, ntb):
    b = pl.program_id(0)
    tb = pl.program_id(1)
    starts = []
    for e in range(ne):
        s0 = w0_ref[(b * ntb + tb) * ne + e]
        sa = jnp.minimum((s0 // BF16_ROWS) * BF16_ROWS, cap - win)
        sa = pl.multiple_of(sa, BF16_ROWS)
        starts.append(sa)
        pltpu.make_async_copy(y_hbm.at[b, e, pl.ds(sa, win)], ybuf.at[e], sem.at[e]).start()
    acc = DN_ALPHA * x1_ref[...]
    lane = lax.broadcasted_iota(I32, (tb_tokens, win), 1).astype(F32)
    for e in range(ne):
        pltpu.make_async_copy(y_hbm.at[b, e, pl.ds(starts[e], win)], ybuf.at[e], sem.at[e]).wait()
        rel = pos_ref[0, :, e:e + 1] - starts[e].astype(F32)
        onehot = (rel == lane).astype(BF16)
        acc = acc + gate_ref[0, :, e:e + 1] * jnp.dot(onehot, ybuf[e], preferred_element_type=F32)
    o_ref[...] = _layer_norm(acc, lg_ref[...], lb_ref[...])


def _combine(y, x1, pos_t, gate_t, w0, ln_g, ln_b, bsz, seq):
    _, ne, cap, d = y.shape
    tbt = min(256, cap)
    win = min(tbt + BF16_ROWS, cap)
    ntb = seq // tbt
    grid_spec = pltpu.PrefetchScalarGridSpec(
        num_scalar_prefetch=1,
        grid=(bsz, ntb),
        in_specs=[pl.BlockSpec(memory_space=pl.ANY),
                  pl.BlockSpec((tbt, d), lambda b, t, w: (b * ntb + t, 0)),
                  pl.BlockSpec((1, tbt, ne), lambda b, t, w: (b, t, 0)),
                  pl.BlockSpec((1, tbt, ne), lambda b, t, w: (b, t, 0)),
                  pl.BlockSpec((1, d), lambda b, t, w: (0, 0)),
                  pl.BlockSpec((1, d), lambda b, t, w: (0, 0))],
        out_specs=pl.BlockSpec((tbt, d), lambda b, t, w: (b * ntb + t, 0)),
        scratch_shapes=[pltpu.VMEM((ne, win, d), BF16), pltpu.SemaphoreType.DMA((ne,))],
    )
    return pl.pallas_call(
        functools.partial(_combine_kernel, tb_tokens=tbt, win=win, cap=cap, ne=ne, ntb=ntb),
        grid_spec=grid_spec,
        out_shape=jax.ShapeDtypeStruct((bsz * seq, d), F32),
        compiler_params=_cparams(2),
        name="moe_combine_ln",
    )(w0, y, x1, pos_t, gate_t, ln_g, ln_b)


def _deinterleave(width):
    idx = np.arange(width).reshape(-1, HEAD_DIM // 2, 2)
    return np.concatenate([idx[:, :, 0], idx[:, :, 1]], axis=1).reshape(-1)


def kernel(x, w_in, b_in, hy_conv_w, hy_conv_b, hy_ffn_w1, hy_ffn_b1, hy_sin_f1, hy_ffn_w2, hy_ffn_b2, hy_sin_f2, hy_ffn_w3, hy_decay, hy_skip, q_norm, k_norm, g_hy, g_attn, w_out, ln1_g, ln1_b, w_router, b_router, w_gate, w_up, w_down, ln2_g, ln2_b):
    bsz, seq, d = x.shape
    n = bsz * seq
    x2 = x.reshape(n, d)
    s1 = 3 * HY_WIDTH
    qw = N_HEADS * HEAD_DIM
    kw = N_KV_HEADS * HEAD_DIM
    perm = np.concatenate([np.arange(s1), s1 + _deinterleave(qw), s1 + qw + _deinterleave(kw),
                           np.arange(s1 + qw + kw, w_in.shape[1])])
    w_in_bf = w_in[:, perm].astype(BF16)
    b_in_p = b_in[perm].reshape(1, -1)
    hd_perm = _deinterleave(HEAD_DIM)

    proj = _in_projection(x2, w_in_bf, b_in_p)

    hv, hx1, hx2 = _short_conv(proj, hy_conv_w, hy_conv_b.reshape(1, -1), seq)
    filt = _hyena_filters(seq, hy_ffn_w1, hy_ffn_b1, hy_sin_f1, hy_ffn_w2, hy_ffn_b2, hy_sin_f2,
                          hy_ffn_w3, hy_decay)
    y_hy = _hyena_mix(hv, hx1, hx2, filt, hy_skip, bsz, seq)

    qt, kb, vt = _qk_prep(proj, q_norm[hd_perm].reshape(1, HEAD_DIM), k_norm[hd_perm].reshape(1, HEAD_DIM), seq)
    y_att = _flash_attention(qt, kb, vt, bsz, seq)

    ne = w_router.shape[1]
    wr_pad = jnp.zeros((d, LANES), F32).at[:, :ne].set(w_router)
    br_pad = jnp.full((1, LANES), -1e30, F32).at[0, :ne].set(b_router)
    x1, aff = _merge(y_hy, y_att, x2, g_hy.reshape(1, -1), g_attn.reshape(1, -1),
                     w_out.astype(BF16), ln1_g.reshape(1, d), ln1_b.reshape(1, d), wr_pad, br_pad)

    cap = CAPACITY_FACTOR * seq // ne
    aff4 = aff[:, :ne].reshape(bsz, seq, ne).transpose(0, 2, 1).reshape(bsz, ne, seq // LANES, LANES)
    idx, pos, gate, excl = _select(aff4, cap)
    wgu_bf = jnp.concatenate([w_gate, w_up], axis=2).astype(BF16)
    y = _experts(idx, x1, wgu_bf, w_down.astype(BF16), bsz, seq)
    pos_t = pos.reshape(bsz, ne, seq).transpose(0, 2, 1)
    gate_t = gate.reshape(bsz, ne, seq).transpose(0, 2, 1)
    tbt = min(256, cap)
    w0 = excl.reshape(bsz, ne, seq // tbt, tbt)[:, :, :, 0].transpose(0, 2, 1).reshape(-1).astype(I32)
    out = _combine(y, x1, pos_t, gate_t, w0, ln2_g.reshape(1, d), ln2_b.reshape(1, d), bsz, seq)
    return out.reshape(bsz, seq, d)
```

```python
import functools
import math

import numpy as np
import jax
import jax.numpy as jnp
from jax import lax
from jax.experimental import pallas as pl
from jax.experimental.pallas import tpu as pltpu

F32 = jnp.float32
BF16 = jnp.bfloat16
I32 = jnp.int32
HIGHEST = lax.Precision.HIGHEST

HY_WIDTH = 1024
HY_BANDS = 16
HEAD_DIM = 128
N_HEADS = 8
N_KV_HEADS = 2
Q_PER_KV = N_HEADS // N_KV_HEADS
ROPE_THETA = 10000.0
GRID_W = 64
N_EXPERTS = 16
CAPACITY_FACTOR = 2
NORM_EPS = 1e-6
DN_ALPHA = 2.0 ** 0.25
LOG2_E = math.log2(math.e)

LANES = 128
SUBLANES = 8
BF16_ROWS = 16
FFT_N1 = 128
VMEM_LIMIT = 56 * 1024 * 1024


def _cparams(n_axes, vmem=VMEM_LIMIT):
    return pltpu.CompilerParams(dimension_semantics=("arbitrary",) * n_axes, vmem_limit_bytes=vmem)


def _round_up(a, m):
    return (a + m - 1) // m * m


def _inproj_kernel(x_ref, w_ref, b_ref, o_ref, xb_ref):
    @pl.when(pl.program_id(1) == 0)
    def _():
        xb_ref[...] = x_ref[...].astype(BF16)

    o_ref[...] = jnp.dot(xb_ref[...], w_ref[...], preferred_element_type=F32) + b_ref[...]


def _in_projection(x2, w_bf, b_row):
    n, d = x2.shape
    wdt = w_bf.shape[1]
    tm = min(1024, n)
    tn = 512
    return pl.pallas_call(
        _inproj_kernel,
        grid=(n // tm, wdt // tn),
        in_specs=[pl.BlockSpec((tm, d), lambda i, j: (i, 0)),
                  pl.BlockSpec((d, tn), lambda i, j: (0, j)),
                  pl.BlockSpec((1, tn), lambda i, j: (0, j))],
        out_specs=pl.BlockSpec((tm, tn), lambda i, j: (i, j)),
        out_shape=jax.ShapeDtypeStruct((n, wdt), F32),
        scratch_shapes=[pltpu.VMEM((tm, d), BF16)],
        compiler_params=_cparams(2),
        name="in_projection",
    )(x2, w_bf, b_row)


def _shortconv_kernel(*refs, tt, seq):
    mains, prevs, nexts = refs[0:3], refs[3:6], refs[6:9]
    w_refs, b_refs, outs = refs[9:12], refs[12:15], refs[15:18]
    i = pl.program_id(0)
    r_in = lax.broadcasted_iota(I32, (tt, 1), 0)
    tpos = (i * tt + r_in) % seq
    for m, p, nx, w, b, o in zip(mains, prevs, nexts, w_refs, b_refs, outs):
        cur = m[...]
        up = pltpu.roll(cur, 1, axis=0)
        up = jnp.where(r_in == 0, p[SUBLANES - 1:SUBLANES, :], up)
        up = jnp.where(tpos == 0, 0.0, up)
        dn = pltpu.roll(cur, tt - 1, axis=0)
        dn = jnp.where(r_in == tt - 1, nx[0:1, :], dn)
        dn = jnp.where(tpos == seq - 1, 0.0, dn)
        wv = w[...]
        o[...] = wv[0:1] * up + wv[1:2] * cur + wv[2:3] * dn + b[...]


def _short_conv(proj, conv_w, conv_b, seq):
    n = proj.shape[0]
    tt = min(512, seq)
    tc = 512
    ncb = HY_WIDTH // tc
    nrb = n // SUBLANES
    hb = tt // SUBLANES
    main = [pl.BlockSpec((tt, tc), functools.partial(lambda i, j, p: (i, p * ncb + j), p=p)) for p in range(3)]
    prev = [pl.BlockSpec((SUBLANES, tc),
                         functools.partial(lambda i, j, p: (jnp.maximum(i * hb - 1, 0), p * ncb + j), p=p))
            for p in range(3)]
    nxt = [pl.BlockSpec((SUBLANES, tc),
                        functools.partial(lambda i, j, p: (jnp.minimum((i + 1) * hb, nrb - 1), p * ncb + j), p=p))
           for p in range(3)]
    wsp = [pl.BlockSpec((3, tc), functools.partial(lambda i, j, p: (0, p * ncb + j), p=p)) for p in range(3)]
    bsp = [pl.BlockSpec((1, tc), functools.partial(lambda i, j, p: (0, p * ncb + j), p=p)) for p in range(3)]
    osp = [pl.BlockSpec((tt, tc), lambda i, j: (i, j)) for _ in range(3)]
    return pl.pallas_call(
        functools.partial(_shortconv_kernel, tt=tt, seq=seq),
        grid=(n // tt, ncb),
        in_specs=main + prev + nxt + wsp + bsp,
        out_specs=osp,
        out_shape=[jax.ShapeDtypeStruct((n, HY_WIDTH), F32)] * 3,
        compiler_params=_cparams(2),
        name="short_conv",
    )(proj, proj, proj, proj, proj, proj, proj, proj, proj,
      conv_w, conv_w, conv_w, conv_b, conv_b, conv_b)


def _filter_kernel(freq_ref, w1_ref, b1_ref, f1_ref, w2_ref, b2_ref, f2_ref, w3_ref, dec_ref, o_ref, *, tt, seq):
    i = pl.program_id(0)
    r = (i * tt + lax.broadcasted_iota(I32, (tt, LANES), 0)).astype(F32)
    lane = lax.broadcasted_iota(I32, (tt, LANES), 1)
    t = r / float(seq - 1)
    w = (2.0 * math.pi / seq) * r
    arg = freq_ref[...] * w
    emb = jnp.where(lane == 0, t,
                    jnp.where(lane <= HY_BANDS, jnp.cos(arg),
                              jnp.where(lane <= 2 * HY_BANDS, -jnp.sin(arg), 0.0)))
    h = jnp.sin(f1_ref[...] * (jnp.dot(emb, w1_ref[...], preferred_element_type=F32, precision=HIGHEST)
                               + b1_ref[...]))
    h = jnp.sin(f2_ref[...] * (jnp.dot(h, w2_ref[...], preferred_element_type=F32, precision=HIGHEST)
                               + b2_ref[...]))
    h3 = jnp.dot(h, w3_ref[...], preferred_element_type=F32, precision=HIGHEST)
    win = jnp.exp(-t[:, 0:1] * jnp.abs(dec_ref[...]))
    out = h3 * win
    col = lax.broadcasted_iota(I32, out.shape, 1)
    row = i * tt + lax.broadcasted_iota(I32, out.shape, 0)
    o_ref[...] = jnp.where((row == 0) & (col >= out.shape[1] // 2), 0.0, out)


def _hyena_filters(seq, w1, b1, f1, w2, b2, f2, w3, decay):
    ffn = w1.shape[1]
    nout = w3.shape[1]
    tt = min(256, seq)
    band = jnp.linspace(1e-4, HY_BANDS - 1, HY_BANDS, dtype=F32)
    freq = jnp.zeros((1, LANES), F32).at[0, 1:1 + HY_BANDS].set(band).at[0, 1 + HY_BANDS:1 + 2 * HY_BANDS].set(band)
    w1p = jnp.zeros((LANES, ffn), F32).at[:w1.shape[0]].set(w1)
    full = lambda shape: pl.BlockSpec(shape, lambda i: (0,) * len(shape))
    return pl.pallas_call(
        functools.partial(_filter_kernel, tt=tt, seq=seq),
        grid=(seq // tt,),
        in_specs=[full((1, LANES)), full((LANES, ffn)), full((1, ffn)), full((1, ffn)),
                  full((ffn, ffn)), full((1, ffn)), full((1, ffn)), full((ffn, nout)), full((1, nout))],
        out_specs=pl.BlockSpec((tt, nout), lambda i: (i, 0)),
        out_shape=jax.ShapeDtypeStruct((seq, nout), F32),
        compiler_params=_cparams(1),
        name="hyena_filters",
    )(freq, w1p, b1.reshape(1, ffn), f1.reshape(1, ffn), w2, b2.reshape(1, ffn), f2.reshape(1, ffn),
      w3, decay.reshape(1, nout))


def _fft_tables(seq):
    n2h = seq // FFT_N1
    n2 = 2 * n2h
    n = FFT_N1 * n2
    k2n = n2h + 1
    k2p = _round_up(k2n, BF16_ROWS)
    kk = np.arange(k2p)[:, None]
    nn = np.arange(n2h)[None, :]
    ang = 2.0 * np.pi * ((kk * nn) % n2) / n2
    valid = (kk < k2n).astype(np.float64)
    fa = np.concatenate([np.cos(ang) * valid, -np.sin(ang) * valid], axis=0)
    wk = np.where((kk == 0) | (kk == n2h), 1.0, 2.0) * valid
    gc = np.concatenate([(np.cos(ang) * wk).T, (-np.sin(ang) * wk).T], axis=1)
    n1 = np.arange(FFT_N1)[None, :]
    tw_ang = 2.0 * np.pi * ((kk * n1) % n) / n
    twr = (np.cos(tw_ang) * valid).reshape(k2p, 1, FFT_N1)
    twi = (-np.sin(tw_ang) * valid).reshape(k2p, 1, FFT_N1)
    k1 = np.arange(FFT_N1)[:, None]
    base = 2.0 * np.pi * ((k1 * n1) % FFT_N1) / FFT_N1
    return dict(n2h=n2h, k2p=k2p, n=n,
                fa=jnp.asarray(fa, BF16), gc=jnp.asarray(gc, BF16),
                twr=jnp.asarray(twr, F32), twi=jnp.asarray(twi, F32),
                cr=jnp.asarray(np.cos(base), F32), ci=jnp.asarray(-np.sin(base), F32))


def _ffta_kernel(x_ref, fa_ref, o_ref):
    xb = x_ref[0].astype(BF16)
    o_ref[0] = jnp.dot(fa_ref[...], xb, preferred_element_type=F32).astype(BF16)


def _fft_stage_a(xv, fa):
    bx, n2h, cols = xv.shape
    rows = fa.shape[0]
    cb = min(2048, cols)
    return pl.pallas_call(
        _ffta_kernel,
        grid=(bx, cols // cb),
        in_specs=[pl.BlockSpec((1, n2h, cb), lambda b, j: (b, 0, j)),
                  pl.BlockSpec((rows, n2h), lambda b, j: (0, 0))],
        out_specs=pl.BlockSpec((1, rows, cb), lambda b, j: (b, 0, j)),
        out_shape=jax.ShapeDtypeStruct((bx, rows, cols), BF16),
        compiler_params=_cparams(2),
        name="fft_stage_a",
    )(xv, fa)


def _stage_b_matrix(twr_ref, twi_ref, cr_ref, ci_ref):
    tr = twr_ref[0]
    ti = twi_ref[0]
    cr = cr_ref[...]
    ci = ci_ref[...]
    mr = cr * tr - ci * ti
    mi = ci * tr + cr * ti
    return mr, mi


def _fftb_filter_kernel(a_ref, twr_ref, twi_ref, cr_ref, ci_ref, o_ref, *, scale):
    mr, mi = _stage_b_matrix(twr_ref, twi_ref, cr_ref, ci_ref)
    fb = jnp.concatenate([jnp.concatenate([mr, -mi], axis=1),
                          jnp.concatenate([mi, mr], axis=1)], axis=0).astype(BF16)
    a = jnp.concatenate([a_ref[0, 0, 0], a_ref[0, 1, 0]], axis=0)
    h = jnp.dot(fb, a, preferred_element_type=F32)
    half = h.shape[1] // 2
    hr, hi = h[:FFT_N1], h[FFT_N1:]
    o_ref[0, 0] = (hr[:, :half] + hr[:, half:]) * scale
    o_ref[0, 1] = (hi[:, :half] - hi[:, half:]) * scale


def _fft_stage_b_filter(a5, tabs, scale):
    k2p = tabs["k2p"]
    c4 = a5.shape[-1]
    return pl.pallas_call(
        functools.partial(_fftb_filter_kernel, scale=scale),
        grid=(k2p,),
        in_specs=[pl.BlockSpec((1, 2, 1, FFT_N1, c4), lambda k: (0, 0, k, 0, 0)),
                  pl.BlockSpec((1, 1, FFT_N1), lambda k: (k, 0, 0)),
                  pl.BlockSpec((1, 1, FFT_N1), lambda k: (k, 0, 0)),
                  pl.BlockSpec((FFT_N1, FFT_N1), lambda k: (0, 0)),
                  pl.BlockSpec((FFT_N1, FFT_N1), lambda k: (0, 0))],
        out_specs=pl.BlockSpec((1, 2, FFT_N1, c4 // 2), lambda k: (k, 0, 0, 0)),
        out_shape=jax.ShapeDtypeStruct((k2p, 2, FFT_N1, c4 // 2), F32),
        compiler_params=_cparams(1),
        name="fft_stage_b_filter",
    )(a5, tabs["twr"], tabs["twi"], tabs["cr"], tabs["ci"])


def _fftb_conv_kernel(a_ref, g_ref, twr_ref, twi_ref, cr_ref, ci_ref, o_ref):
    mr, mi = _stage_b_matrix(twr_ref, twi_ref, cr_ref, ci_ref)
    fb = jnp.concatenate([jnp.concatenate([mr, -mi], axis=1),
                          jnp.concatenate([mi, mr], axis=1)], axis=0).astype(BF16)
    mrt, mit = mr.T, mi.T
    fbt = jnp.concatenate([jnp.concatenate([mrt, mit], axis=1),
                           jnp.concatenate([-mit, mrt], axis=1)], axis=0).astype(BF16)
    gr = g_ref[0, 0]
    gi = g_ref[0, 1]
    for b in range(a_ref.shape[0]):
        a = jnp.concatenate([a_ref[b, 0, 0], a_ref[b, 1, 0]], axis=0)
        xh = jnp.dot(fb, a, preferred_element_type=F32)
        xr, xi = xh[:FFT_N1], xh[FFT_N1:]
        y = jnp.concatenate([xr * gr - xi * gi, xr * gi + xi * gr], axis=0).astype(BF16)
        c = jnp.dot(fbt, y, preferred_element_type=F32)
        o_ref[b, 0, 0] = c[:FFT_N1].astype(BF16)
        o_ref[b, 1, 0] = c[FFT_N1:].astype(BF16)


def _fft_stage_b_conv(a5, g, order, tabs):
    bsz, _, k2p, _, c = a5.shape
    return pl.pallas_call(
        _fftb_conv_kernel,
        grid=(k2p,),
        in_specs=[pl.BlockSpec((bsz, 2, 1, FFT_N1, c), lambda k: (0, 0, k, 0, 0)),
                  pl.BlockSpec((1, 2, FFT_N1, c), lambda k: (k, 0, 0, order)),
                  pl.BlockSpec((1, 1, FFT_N1), lambda k: (k, 0, 0)),
                  pl.BlockSpec((1, 1, FFT_N1), lambda k: (k, 0, 0)),
                  pl.BlockSpec((FFT_N1, FFT_N1), lambda k: (0, 0)),
                  pl.BlockSpec((FFT_N1, FFT_N1), lambda k: (0, 0))],
        out_specs=pl.BlockSpec((bsz, 2, 1, FFT_N1, c), lambda k: (0, 0, k, 0, 0)),
        out_shape=jax.ShapeDtypeStruct(a5.shape, BF16),
        compiler_params=_cparams(1),
        name="fft_stage_b_conv",
    )(a5, g, tabs["twr"], tabs["twi"], tabs["cr"], tabs["ci"])


def _fftc_kernel(c_ref, gc_ref, u_ref, gate_ref, d_ref, o_ref):
    y = jnp.dot(gc_ref[...], c_ref[0], preferred_element_type=F32)
    o_ref[0] = gate_ref[0] * (y + d_ref[...] * u_ref[0])


def _fft_stage_c(cm, gc, uv, gatev, d_blk):
    bsz, rows, cols = cm.shape
    n2h = uv.shape[1]
    cb = d_blk.shape[1]
    return pl.pallas_call(
        _fftc_kernel,
        grid=(bsz, cols // cb),
        in_specs=[pl.BlockSpec((1, rows, cb), lambda b, j: (b, 0, j)),
                  pl.BlockSpec((n2h, rows), lambda b, j: (0, 0)),
                  pl.BlockSpec((1, n2h, cb), lambda b, j: (b, 0, j)),
                  pl.BlockSpec((1, n2h, cb), lambda b, j: (b, 0, j)),
                  pl.BlockSpec((1, cb), lambda b, j: (0, 0))],
        out_specs=pl.BlockSpec((1, n2h, cb), lambda b, j: (b, 0, j)),
        out_shape=jax.ShapeDtypeStruct(uv.shape, F32),
        compiler_params=_cparams(2),
        name="fft_stage_c",
    )(cm, gc, uv, gatev, d_blk)


def _hyena_mix(hv, hx1, hx2, filt, skip, bsz, seq):
    c = hv.shape[1]
    tabs = _fft_tables(seq)
    n2h, k2p, n = tabs["n2h"], tabs["k2p"], tabs["n"]
    cols = FFT_N1 * c
    cb = min(2048, cols)
    fa_f = _fft_stage_a(filt.reshape(1, n2h, FFT_N1 * 4 * c), tabs["fa"])
    g = _fft_stage_b_filter(fa_f.reshape(1, 2, k2p, FFT_N1, 4 * c), tabs, 1.0 / n)

    def conv(u, gate, order):
        a = _fft_stage_a(u.reshape(bsz, n2h, cols), tabs["fa"])
        cm = _fft_stage_b_conv(a.reshape(bsz, 2, k2p, FFT_N1, c), g, order, tabs)
        d_blk = jnp.tile(skip[order].reshape(1, c), (1, cb // c))
        z = _fft_stage_c(cm.reshape(bsz, 2 * k2p, cols), tabs["gc"], u.reshape(bsz, n2h, cols),
                         gate.reshape(bsz, n2h, cols), d_blk)
        return z.reshape(bsz * seq, c)

    z = conv(hv, hx1, 0)
    return conv(z, hx2, 1)


def _qkprep_kernel(q_ref, k_ref, v_ref, qn_ref, kn_ref, oqt_ref, ok_ref, ovt_ref, *, tt, seq):
    i = pl.program_id(0)
    t = (i * tt + lax.broadcasted_iota(I32, (tt, HEAD_DIM), 0)) % seq
    lane = lax.broadcasted_iota(I32, (tt, HEAD_DIM), 1)
    j = lane % (HEAD_DIM // 2)
    quarter = HEAD_DIM // 4
    f = (j % quarter).astype(F32)
    inv = jnp.exp(f * (-2.0 * math.log(ROPE_THETA) / (HEAD_DIM // 2)))
    pos = jnp.where(j < quarter, t // GRID_W, t % GRID_W).astype(F32)
    ang = pos * inv
    cosv = jnp.cos(ang)
    sinv = jnp.sin(ang)
    sin_signed = jnp.where(lane < HEAD_DIM // 2, -sinv, sinv)

    def norm_rope(x, g, scale):
        ms = jnp.mean(x * x, axis=-1, keepdims=True)
        xn = x * lax.rsqrt(ms + NORM_EPS) * g
        return (xn * cosv + pltpu.roll(xn, HEAD_DIM // 2, axis=1) * sin_signed) * scale

    qg = qn_ref[...]
    kg = kn_ref[...]
    q_scale = LOG2_E * HEAD_DIM ** -0.5
    for h in range(N_HEADS):
        sl = slice(h * HEAD_DIM, (h + 1) * HEAD_DIM)
        oqt_ref[h] = norm_rope(q_ref[:, sl], qg, q_scale).T.astype(BF16)
    for h in range(N_KV_HEADS):
        sl = slice(h * HEAD_DIM, (h + 1) * HEAD_DIM)
        ok_ref[:, sl] = norm_rope(k_ref[:, sl], kg, 1.0).astype(BF16)
        ovt_ref[h] = v_ref[:, sl].T.astype(BF16)


def _qk_prep(proj, qn, kn, seq):
    n = proj.shape[0]
    tt = min(512, seq)
    qw = N_HEADS * HEAD_DIM
    kw = N_KV_HEADS * HEAD_DIM
    q_off = (3 * HY_WIDTH) // qw
    k_off = (3 * HY_WIDTH + qw) // kw
    return pl.pallas_call(
        functools.partial(_qkprep_kernel, tt=tt, seq=seq),
        grid=(n // tt,),
        in_specs=[pl.BlockSpec((tt, qw), lambda i: (i, q_off)),
                  pl.BlockSpec((tt, kw), lambda i: (i, k_off)),
                  pl.BlockSpec((tt, kw), lambda i: (i, k_off + 1)),
                  pl.BlockSpec((1, HEAD_DIM), lambda i: (0, 0)),
                  pl.BlockSpec((1, HEAD_DIM), lambda i: (0, 0))],
        out_specs=[pl.BlockSpec((N_HEADS, HEAD_DIM, tt), lambda i: (0, 0, i)),
                   pl.BlockSpec((tt, kw), lambda i: (i, 0)),
                   pl.BlockSpec((N_KV_HEADS, HEAD_DIM, tt), lambda i: (0, 0, i))],
        out_shape=[jax.ShapeDtypeStruct((N_HEADS, HEAD_DIM, n), BF16),
                   jax.ShapeDtypeStruct((n, kw), BF16),
                   jax.ShapeDtypeStruct((N_KV_HEADS, HEAD_DIM, n), BF16)],
        compiler_params=_cparams(1),
        name="qk_prep",
    )(proj, proj, proj, qn, kn)


def _flash_kernel(qt_ref, k_ref, vt_ref, o_ref, acc_sc, s_sc, *, tq, tk, seq, unroll):
    nq = Q_PER_KV * tq
    nk = seq // tk
    qt = jnp.concatenate([qt_ref[g] for g in range(Q_PER_KV)], axis=1)
    acc_sc[...] = jnp.zeros(acc_sc.shape, F32)

    def scores(c):
        start = pl.multiple_of(c * tk, tk)
        return jnp.dot(k_ref[pl.ds(start, tk), :], qt, preferred_element_type=F32)

    s_sc[0] = scores(0)

    def body(i, carry):
        m, l = carry
        for u in range(unroll):
            c = i * unroll + u
            s_sc[(u + 1) % 2] = scores(jnp.minimum(c + 1, nk - 1))
            s = s_sc[u % 2]
            start = pl.multiple_of(c * tk, tk)
            vc = vt_ref[0, :, pl.ds(start, tk)]
            mc = jnp.max(s, axis=0, keepdims=True)
            p = jnp.exp2(s - mc)
            lc = jnp.sum(p, axis=0, keepdims=True)
            oc = jnp.dot(vc, p.astype(BF16), preferred_element_type=F32)
            m_new = jnp.maximum(m, mc)
            a = jnp.exp2(m - m_new)
            bc = jnp.exp2(mc - m_new)
            l = l * a + lc * bc
            acc_sc[...] = acc_sc[...] * a + oc * bc
            m = m_new
        return m, l

    m0 = jnp.full((1, nq), -jnp.inf, F32)
    l0 = jnp.zeros((1, nq), F32)
    _, l = lax.fori_loop(0, nk // unroll, body, (m0, l0))
    o = acc_sc[...] / l
    for g in range(Q_PER_KV):
        o_ref[:, g * HEAD_DIM:(g + 1) * HEAD_DIM] = o[:, g * tq:(g + 1) * tq].T


def _flash_attention(qt, k, vt, bsz, seq):
    n = bsz * seq
    tq = min(256, seq)
    tk = min(1024, seq // 2)
    unroll = 2 if seq // tk >= 2 else 1
    gw = Q_PER_KV * HEAD_DIM
    nqb = seq // tq
    return pl.pallas_call(
        functools.partial(_flash_kernel, tq=tq, tk=tk, seq=seq, unroll=unroll),
        grid=(bsz, N_KV_HEADS, nqb),
        in_specs=[pl.BlockSpec((Q_PER_KV, HEAD_DIM, tq), lambda b, h, i: (h, 0, b * nqb + i)),
                  pl.BlockSpec((seq, HEAD_DIM), lambda b, h, i: (b, h)),
                  pl.BlockSpec((1, HEAD_DIM, seq), lambda b, h, i: (h, 0, b))],
        out_specs=pl.BlockSpec((tq, gw), lambda b, h, i: (b * nqb + i, h)),
        out_shape=jax.ShapeDtypeStruct((n, N_HEADS * HEAD_DIM), F32),
        scratch_shapes=[pltpu.VMEM((HEAD_DIM, Q_PER_KV * tq), F32),
                        pltpu.VMEM((2, tk, Q_PER_KV * tq), F32)],
        compiler_params=_cparams(3),
        name="flash_attention",
    )(qt, k, vt)


def _layer_norm(y, g, b):
    mu = jnp.mean(y, axis=-1, keepdims=True)
    yc = y - mu
    var = jnp.mean(yc * yc, axis=-1, keepdims=True)
    return yc * lax.rsqrt(var + NORM_EPS) * g + b


def _merge_kernel(yh_ref, ya_ref, x_ref, gh_ref, ga_ref, wo_ref, lg_ref, lb_ref, wr_ref, br_ref,
                  x1_ref, aff_ref):
    hw = yh_ref.shape[1]
    yh = yh_ref[...]
    ya = ya_ref[...]
    mh = yh * lax.rsqrt(jnp.mean(yh * yh, axis=-1, keepdims=True) + NORM_EPS) * gh_ref[...]
    ma = ya * lax.rsqrt(jnp.mean(ya * ya, axis=-1, keepdims=True) + NORM_EPS) * ga_ref[...]
    y = (jnp.dot(mh.astype(BF16), wo_ref[:hw, :], preferred_element_type=F32)
         + jnp.dot(ma.astype(BF16), wo_ref[hw:, :], preferred_element_type=F32)
         + DN_ALPHA * x_ref[...])
    x1 = _layer_norm(y, lg_ref[...], lb_ref[...])
    x1_ref[...] = x1
    x_hi = x1.astype(BF16)
    x_lo = (x1 - x_hi.astype(F32)).astype(BF16)
    both = jnp.dot(x_hi, wr_ref[...], preferred_element_type=F32)
    logits = (both[:, :LANES] + both[:, LANES:]
              + jnp.dot(x_lo, wr_ref[:, :LANES], preferred_element_type=F32) + br_ref[...])
    e = jnp.exp(logits - jnp.max(logits, axis=-1, keepdims=True))
    aff_ref[...] = e / jnp.sum(e, axis=-1, keepdims=True)


def _merge(y_hy, y_att, x2, g_hy, g_attn, wo_bf, ln_g, ln_b, wr_pad, br_pad):
    n, d = x2.shape
    hw = y_hy.shape[1]
    tm = min(256, n)
    row = lambda w: pl.BlockSpec((1, w), lambda i: (0, 0))
    return pl.pallas_call(
        _merge_kernel,
        grid=(n // tm,),
        in_specs=[pl.BlockSpec((tm, hw), lambda i: (i, 0)),
                  pl.BlockSpec((tm, d - hw), lambda i: (i, 0)),
                  pl.BlockSpec((tm, d), lambda i: (i, 0)),
                  row(hw), row(d - hw),
                  pl.BlockSpec((d, d), lambda i: (0, 0)),
                  row(d), row(d),
                  pl.BlockSpec((d, 2 * LANES), lambda i: (0, 0)),
                  row(LANES)],
        out_specs=[pl.BlockSpec((tm, d), lambda i: (i, 0)),
                   pl.BlockSpec((tm, LANES), lambda i: (i, 0))],
        out_shape=[jax.ShapeDtypeStruct((n, d), F32), jax.ShapeDtypeStruct((n, LANES), F32)],
        compiler_params=_cparams(1),
        name="merge_ln_router",
    )(y_hy, y_att, x2, g_hy, g_attn, wo_bf, ln_g, ln_b, wr_pad, br_pad)


def _select_kernel(a_ref, idx_ref, gslot_ref, pos_ref, excl_ref, *, cap):
    a = a_ref[0]
    ne, nr, _ = a.shape
    bits = lax.bitcast_convert_type(a, I32)

    def count(mask):
        return jnp.sum(jnp.sum(mask.astype(F32), axis=2, keepdims=True), axis=1, keepdims=True)

    thr = jnp.zeros((ne, 1, 1), I32)
    for bit in range(30, -1, -1):
        cand = thr | (1 << bit)
        thr = jnp.where(count(bits >= cand) >= cap, cand, thr)

    ii = lax.broadcasted_iota(I32, (LANES, LANES), 0)
    jj = lax.broadcasted_iota(I32, (LANES, LANES), 1)
    upper = (ii <= jj).astype(BF16)
    ri = lax.broadcasted_iota(I32, (ne, nr, nr), 1)
    rj = lax.broadcasted_iota(I32, (ne, nr, nr), 2)
    lower_strict = (rj < ri).astype(BF16)

    def prefix(mask):
        mb = mask.astype(BF16).reshape(ne * nr, LANES)
        pin = jnp.dot(mb, upper, preferred_element_type=F32).reshape(ne, nr, LANES)
        tot = pin[:, :, LANES - 1:LANES]
        totb = jnp.broadcast_to(tot, (ne, nr, LANES)).astype(BF16)
        off = jnp.einsum("ers,esl->erl", lower_strict, totb, preferred_element_type=F32)
        return pin, tot, off[:, :, 0:1]

    gt = bits > thr
    eq = bits == thr
    need = cap - count(gt)
    pin_eq, _, off_eq = prefix(eq)
    eq_rank = pin_eq + off_eq - eq.astype(F32)
    chosen = gt | (eq & (eq_rank < need))
    pin, tot, off = prefix(chosen)
    chosen_f = chosen.astype(F32)
    excl = pin + off - chosen_f
    excl_ref[0] = excl
    pos_ref[0] = jnp.where(chosen, excl, -1.0)

    s_row = lax.broadcasted_iota(I32, (nr, cap), 1).astype(F32)
    r_col = lax.broadcasted_iota(I32, (nr, cap), 0).astype(F32)
    j_col = lax.broadcasted_iota(I32, (LANES, cap), 0).astype(F32)
    for e in range(ne):
        lo = off[e]
        hi = lo + tot[e]
        in_row = ((lo <= s_row) & (s_row < hi)).astype(F32)
        in_row_b = in_row.astype(BF16)
        row_of_s = jnp.sum(in_row * r_col, axis=0, keepdims=True)
        s_local = s_row[0:1] - jnp.sum(in_row * lo, axis=0, keepdims=True)
        pin_t = pin[e].T.astype(BF16)
        g = jnp.dot(pin_t, in_row_b, preferred_element_type=F32)
        lane_of_s = jnp.sum((g <= s_local).astype(F32), axis=0, keepdims=True)
        idx_ref[0, e:e + 1, :] = (row_of_s * LANES + lane_of_s).astype(I32)
        rem = a[e]
        a_of_s = jnp.zeros((LANES, cap), F32)
        for _ in range(3):
            part = rem.astype(BF16)
            rem = rem - part.astype(F32)
            a_of_s = a_of_s + jnp.dot(part.astype(F32).T.astype(BF16), in_row_b, preferred_element_type=F32)
        gslot_ref[0, e:e + 1, :] = jnp.sum(jnp.where(j_col == lane_of_s, a_of_s, 0.0), axis=0, keepdims=True)


def _select(aff4, cap):
    bsz, ne, nr, _ = aff4.shape
    blk = pl.BlockSpec((1, ne, nr, LANES), lambda b: (b, 0, 0, 0))
    slot = pl.BlockSpec((1, ne, cap), lambda b: (b, 0, 0))
    return pl.pallas_call(
        functools.partial(_select_kernel, cap=cap),
        grid=(bsz,),
        in_specs=[blk],
        out_specs=[slot, slot, blk, blk],
        out_shape=[jax.ShapeDtypeStruct((bsz, ne, cap), I32),
                   jax.ShapeDtypeStruct((bsz, ne, cap), F32),
                   jax.ShapeDtypeStruct(aff4.shape, F32),
                   jax.ShapeDtypeStruct(aff4.shape, F32)],
        compiler_params=_cparams(1),
        name="expert_choice_select",
    )(aff4)


def _expert_kernel(idx_ref, idxn_ref, x_hbm, g_ref, wgu_ref, wd_ref, o_ref, xg_ref, sem,
                   *, tc, seq, ff, nj, bsz, nsteps):
    s = pl.program_id(0)
    slot = s % 2

    def issue(ids_ref, step, to_slot):
        base = ((step // nj) % bsz) * seq
        for r in range(tc):
            pltpu.make_async_copy(x_hbm.at[pl.ds(base + ids_ref[0, 0, r], 1)],
                                  xg_ref.at[to_slot, pl.ds(r, 1)], sem.at[to_slot]).start()

    def wait_rows(of_slot):
        pltpu.make_async_copy(x_hbm.at[pl.ds(0, tc)], xg_ref.at[of_slot], sem.at[of_slot]).wait()

    @pl.when(s == 0)
    def _():
        issue(idx_ref, s, 0)

    issue(idxn_ref, jnp.minimum(s + 1, nsteps - 1), 1 - slot)
    wait_rows(slot)
    xb = xg_ref[slot].astype(BF16)
    gu = jnp.dot(xb, wgu_ref[0], preferred_element_type=F32)
    gate = gu[:, :ff]
    h = gate * jax.nn.sigmoid(gate) * gu[:, ff:]
    y = jnp.dot(h.astype(BF16), wd_ref[0], preferred_element_type=F32)
    o_ref[0, 0] = (y * g_ref[0]).astype(BF16)

    @pl.when(s == nsteps - 1)
    def _():
        wait_rows(1 - slot)


def _experts(idx, gslot, x1, wgu_bf, wd_bf, bsz, seq):
    _, ne, cap = idx.shape
    d = x1.shape[1]
    ff = wd_bf.shape[1]
    tc = min(256, cap)
    nj = cap // tc
    nsteps = ne * bsz * nj
    idx_s = idx.transpose(1, 0, 2).reshape(nsteps, 1, tc)
    g_s = gslot.transpose(1, 0, 2).reshape(nsteps, tc, 1)
    out_map = lambda s: ((s // nj) % bsz, s // (bsz * nj), s % nj, 0)
    return pl.pallas_call(
        functools.partial(_expert_kernel, tc=tc, seq=seq, ff=ff, nj=nj, bsz=bsz, nsteps=nsteps),
        grid=(nsteps,),
        in_specs=[pl.BlockSpec((1, 1, tc), lambda s: (s, 0, 0), memory_space=pltpu.SMEM),
                  pl.BlockSpec((1, 1, tc), lambda s: (jnp.minimum(s + 1, nsteps - 1), 0, 0),
                               memory_space=pltpu.SMEM),
                  pl.BlockSpec(memory_space=pl.ANY),
                  pl.BlockSpec((1, tc, 1), lambda s: (s, 0, 0)),
                  pl.BlockSpec((1, d, 2 * ff), lambda s: (s // (bsz * nj), 0, 0)),
                  pl.BlockSpec((1, ff, d), lambda s: (s // (bsz * nj), 0, 0))],
        out_specs=pl.BlockSpec((1, 1, tc, d), out_map),
        out_shape=jax.ShapeDtypeStruct((bsz, ne, cap, d), BF16),
        scratch_shapes=[pltpu.VMEM((2, tc, d), F32), pltpu.SemaphoreType.DMA((2,))],
        compiler_params=_cparams(1),
        name="expert_ffn",
    )(idx_s, idx_s, x1, g_s, wgu_bf, wd_bf)


def _combine_kernel(w0_ref, y_hbm, x1_ref, pos_ref, lg_ref, lb_ref, o_ref, ybuf, xbuf, acc_ref, sem, xsem,
                    *, tb_tokens, win, cap, ne, ntb):
    b = pl.program_id(0)
    tb = pl.program_id(1)
    base = (b * (ntb + 1) + tb) * ne
    starts = []
    for e in range(ne):
        sa = jnp.minimum((w0_ref[base + e] // BF16_ROWS) * BF16_ROWS, cap - win)
        sa = pl.multiple_of(sa, BF16_ROWS)
        starts.append(sa)
        pltpu.make_async_copy(y_hbm.at[b, e, pl.ds(sa, win)], ybuf.at[pl.ds(e * win, win)], sem.at[e]).start()
    lane = lax.broadcasted_iota(I32, (tb_tokens, win), 1).astype(F32)
    onehots = [(pos_ref[0, :, e:e + 1] - starts[e].astype(F32) == lane).astype(BF16) for e in range(ne)]
    p = jnp.concatenate(onehots, axis=1)
    for e in range(ne):
        pltpu.make_async_copy(y_hbm.at[b, e, pl.ds(starts[e], win)], ybuf.at[pl.ds(e * win, win)],
                              sem.at[e]).wait()
    acc_ref[...] = DN_ALPHA * x1_ref[...] + jnp.dot(p, ybuf[...], preferred_element_type=F32)

    for e in range(ne):
        first_end = starts[e] + win
        n_extra = jnp.maximum(w0_ref[base + ne + e] - first_end + win - 1, 0) // win

        def extra(w, carry, e=e, first_end=first_end):
            nominal = first_end + w * win
            st = pl.multiple_of(jnp.minimum(nominal, cap - win), BF16_ROWS)
            cp = pltpu.make_async_copy(y_hbm.at[b, e, pl.ds(st, win)], xbuf, xsem)
            cp.start()
            cp.wait()
            pe = pos_ref[0, :, e:e + 1]
            oh = ((pe - st.astype(F32) == lane) & (pe >= nominal.astype(F32))).astype(BF16)
            acc_ref[...] += jnp.dot(oh, xbuf[...], preferred_element_type=F32)
            return carry

        lax.fori_loop(0, n_extra, extra, 0)
    o_ref[...] = _layer_norm(acc_ref[...], lg_ref[...], lb_ref[...])


def _combine(y, x1, pos_t, w0, ln_g, ln_b, bsz, seq, tbt):
    _, ne, cap, d = y.shape
    win = min(LANES, cap)
    ntb = seq // tbt
    grid_spec = pltpu.PrefetchScalarGridSpec(
        num_scalar_prefetch=1,
        grid=(bsz, ntb),
        in_specs=[pl.BlockSpec(memory_space=pl.ANY),
                  pl.BlockSpec((tbt, d), lambda b, t, w: (b * ntb + t, 0)),
                  pl.BlockSpec((1, tbt, ne), lambda b, t, w: (b, t, 0)),
                  pl.BlockSpec((1, d), lambda b, t, w: (0, 0)),
                  pl.BlockSpec((1, d), lambda b, t, w: (0, 0))],
        out_specs=pl.BlockSpec((tbt, d), lambda b, t, w: (b * ntb + t, 0)),
        scratch_shapes=[pltpu.VMEM((ne * win, d), BF16), pltpu.VMEM((win, d), BF16),
                        pltpu.VMEM((tbt, d), F32),
                        pltpu.SemaphoreType.DMA((ne,)), pltpu.SemaphoreType.DMA(())],
    )
    return pl.pallas_call(
        functools.partial(_combine_kernel, tb_tokens=tbt, win=win, cap=cap, ne=ne, ntb=ntb),
        grid_spec=grid_spec,
        out_shape=jax.ShapeDtypeStruct((bsz * seq, d), F32),
        compiler_params=_cparams(2),
        name="moe_combine_ln",
    )(w0, y, x1, pos_t, ln_g, ln_b)


def _deinterleave(width):
    idx = np.arange(width).reshape(-1, HEAD_DIM // 2, 2)
    return np.concatenate([idx[:, :, 0], idx[:, :, 1]], axis=1).reshape(-1)


def kernel(x, w_in, b_in, hy_conv_w, hy_conv_b, hy_ffn_w1, hy_ffn_b1, hy_sin_f1, hy_ffn_w2, hy_ffn_b2, hy_sin_f2, hy_ffn_w3, hy_decay, hy_skip, q_norm, k_norm, g_hy, g_attn, w_out, ln1_g, ln1_b, w_router, b_router, w_gate, w_up, w_down, ln2_g, ln2_b):
    bsz, seq, d = x.shape
    n = bsz * seq
    x2 = x.reshape(n, d)
    s1 = 3 * HY_WIDTH
    qw = N_HEADS * HEAD_DIM
    kw = N_KV_HEADS * HEAD_DIM
    perm = np.concatenate([np.arange(s1), s1 + _deinterleave(qw), s1 + qw + _deinterleave(kw),
                           np.arange(s1 + qw + kw, w_in.shape[1])])
    w_in_bf = w_in[:, perm].astype(BF16)
    b_in_p = b_in[perm].reshape(1, -1)
    hd_perm = _deinterleave(HEAD_DIM)

    proj = _in_projection(x2, w_in_bf, b_in_p)

    hv, hx1, hx2 = _short_conv(proj, hy_conv_w, hy_conv_b.reshape(1, -1), seq)
    filt = _hyena_filters(seq, hy_ffn_w1, hy_ffn_b1, hy_sin_f1, hy_ffn_w2, hy_ffn_b2, hy_sin_f2,
                          hy_ffn_w3, hy_decay)
    y_hy = _hyena_mix(hv, hx1, hx2, filt, hy_skip, bsz, seq)

    qt, kb, vt = _qk_prep(proj, q_norm[hd_perm].reshape(1, HEAD_DIM), k_norm[hd_perm].reshape(1, HEAD_DIM), seq)
    y_att = _flash_attention(qt, kb, vt, bsz, seq)

    ne = w_router.shape[1]
    wr_hi = w_router.astype(BF16)
    wr_lo = (w_router - wr_hi.astype(F32)).astype(BF16)
    wr_pair = (jnp.zeros((d, 2 * LANES), BF16).at[:, :ne].set(wr_hi).at[:, LANES:LANES + ne].set(wr_lo))
    br_pad = jnp.full((1, LANES), -1e30, F32).at[0, :ne].set(b_router)
    x1, aff = _merge(y_hy, y_att, x2, g_hy.reshape(1, -1), g_attn.reshape(1, -1),
                     w_out.astype(BF16), ln1_g.reshape(1, d), ln1_b.reshape(1, d), wr_pair, br_pad)

    cap = CAPACITY_FACTOR * seq // ne
    aff4 = aff[:, :ne].reshape(bsz, seq, ne).transpose(0, 2, 1).reshape(bsz, ne, seq // LANES, LANES)
    idx, gslot, pos, excl = _select(aff4, cap)
    wgu_bf = jnp.concatenate([w_gate, w_up], axis=2).astype(BF16)
    y = _experts(idx, gslot, x1, wgu_bf, w_down.astype(BF16), bsz, seq)
    pos_t = pos.reshape(bsz, ne, seq).transpose(0, 2, 1)
    tbt = min(256, cap)
    before = excl.reshape(bsz, ne, seq // tbt, tbt)[:, :, :, 0].transpose(0, 2, 1)
    w0 = jnp.concatenate([before, jnp.full((bsz, 1, ne), cap, F32)], axis=1).reshape(-1).astype(I32)
    out = _combine(y, x1, pos_t, w0, ln2_g.reshape(1, d), ln2_b.reshape(1, d), bsz, seq, tbt)
    return out.reshape(bsz, seq, d)
```

```python
import functools
import math

import numpy as np
import jax
import jax.numpy as jnp
from jax import lax
from jax.experimental import pallas as pl
from jax.experimental.pallas import tpu as pltpu

F32 = jnp.float32
BF16 = jnp.bfloat16
I32 = jnp.int32
U32 = jnp.uint32
HIGHEST = lax.Precision.HIGHEST

HY_WIDTH = 1024
HY_BANDS = 16
HEAD_DIM = 128
N_HEADS = 8
N_KV_HEADS = 2
Q_PER_KV = N_HEADS // N_KV_HEADS
ROPE_THETA = 10000.0
GRID_W = 64
N_EXPERTS = 16
CAPACITY_FACTOR = 2
NORM_EPS = 1e-6
DN_ALPHA = 2.0 ** 0.25
LOG2_E = math.log2(math.e)

LANES = 128
SUBLANES = 8
BF16_ROWS = 16
FFT_N1 = 128
VMEM_LIMIT = 56 * 1024 * 1024


def _cparams(n_axes, vmem=VMEM_LIMIT):
    return pltpu.CompilerParams(dimension_semantics=("arbitrary",) * n_axes, vmem_limit_bytes=vmem)


def _round_up(a, m):
    return (a + m - 1) // m * m


def _inproj_kernel(x_ref, w_ref, b_ref, o_ref, xb_ref):
    @pl.when(pl.program_id(1) == 0)
    def _():
        xb_ref[...] = x_ref[...].astype(BF16)

    o_ref[...] = jnp.dot(xb_ref[...], w_ref[...], preferred_element_type=F32) + b_ref[...]


def _in_projection(x2, w_bf, b_row):
    n, d = x2.shape
    wdt = w_bf.shape[1]
    tm = min(1024, n)
    tn = wdt // 3
    return pl.pallas_call(
        _inproj_kernel,
        grid=(n // tm, wdt // tn),
        in_specs=[pl.BlockSpec((tm, d), lambda i, j: (i, 0)),
                  pl.BlockSpec((d, tn), lambda i, j: (0, j)),
                  pl.BlockSpec((1, tn), lambda i, j: (0, j))],
        out_specs=pl.BlockSpec((tm, tn), lambda i, j: (i, j)),
        out_shape=jax.ShapeDtypeStruct((n, wdt), F32),
        scratch_shapes=[pltpu.VMEM((tm, d), BF16)],
        compiler_params=_cparams(2),
        name="in_projection",
    )(x2, w_bf, b_row)


def _store_fft_layout(res, o_ref, scr):
    tt, c = res.shape
    nct = c // LANES
    for ct in range(nct):
        scr[ct] = res[:, ct * LANES:(ct + 1) * LANES]

    def body(n1, carry):
        for ct in range(nct):
            col = pl.multiple_of(n1 * c + ct * LANES, LANES)
            o_ref[0, :, pl.ds(col, LANES)] = scr[ct, pl.ds(n1, tt // FFT_N1, stride=FFT_N1), :]
        return carry

    lax.fori_loop(0, FFT_N1, body, 0)


def _shortconv_kernel(m_ref, p_ref, nx_ref, w_ref, b_ref, o_ref, scr, *, tt, seq):
    i = pl.program_id(0)
    r_in = lax.broadcasted_iota(I32, (tt, 1), 0)
    tpos = (i * tt + r_in) % seq
    cur = m_ref[...]
    up = pltpu.roll(cur, 1, axis=0)
    up = jnp.where(r_in == 0, p_ref[SUBLANES - 1:SUBLANES, :], up)
    up = jnp.where(tpos == 0, 0.0, up)
    dn = pltpu.roll(cur, tt - 1, axis=0)
    dn = jnp.where(r_in == tt - 1, nx_ref[0:1, :], dn)
    dn = jnp.where(tpos == seq - 1, 0.0, dn)
    wv = w_ref[...]
    _store_fft_layout(wv[0:1] * up + wv[1:2] * cur + wv[2:3] * dn + b_ref[...], o_ref, scr)


def _short_conv(proj, conv_w, conv_b, part, bsz, seq):
    n = proj.shape[0]
    c = HY_WIDTH
    tt = SUBLANES * FFT_N1
    nrb = n // SUBLANES
    hb = tt // SUBLANES
    tps = seq // tt
    return pl.pallas_call(
        functools.partial(_shortconv_kernel, tt=tt, seq=seq),
        grid=(n // tt,),
        in_specs=[pl.BlockSpec((tt, c), lambda i: (i, part)),
                  pl.BlockSpec((SUBLANES, c), lambda i: (jnp.maximum(i * hb - 1, 0), part)),
                  pl.BlockSpec((SUBLANES, c), lambda i: (jnp.minimum((i + 1) * hb, nrb - 1), part)),
                  pl.BlockSpec((3, c), lambda i: (0, part)),
                  pl.BlockSpec((1, c), lambda i: (0, part))],
        out_specs=pl.BlockSpec((1, SUBLANES, FFT_N1 * c), lambda i: (i // tps, i % tps, 0)),
        out_shape=jax.ShapeDtypeStruct((bsz, seq // FFT_N1, FFT_N1 * c), F32),
        scratch_shapes=[pltpu.VMEM((c // LANES, tt, LANES), F32)],
        compiler_params=_cparams(1),
        name="short_conv",
    )(proj, proj, proj, conv_w, conv_b)


def _filter_kernel(freq_ref, w1_ref, b1_ref, f1_ref, w2_ref, b2_ref, f2_ref, w3_ref, dec_ref, o_ref, scr, h_sc,
                   *, tt, seq, nparts):
    i = pl.program_id(0)
    part = pl.program_id(1)
    @pl.when(part == 0)
    def _():
        r = (i * tt + lax.broadcasted_iota(I32, (tt, LANES), 0)).astype(F32)
        lane = lax.broadcasted_iota(I32, (tt, LANES), 1)
        t = r / float(seq - 1)
        w = (2.0 * math.pi / seq) * r
        arg = freq_ref[...] * w
        emb = jnp.where(lane == 0, t,
                        jnp.where(lane <= HY_BANDS, jnp.cos(arg),
                                  jnp.where(lane <= 2 * HY_BANDS, -jnp.sin(arg), 0.0)))
        h = jnp.sin(f1_ref[...] * (jnp.dot(emb, w1_ref[...], preferred_element_type=F32, precision=HIGHEST)
                                   + b1_ref[...]))
        h_sc[...] = jnp.sin(f2_ref[...] * (jnp.dot(h, w2_ref[...], preferred_element_type=F32,
                                                   precision=HIGHEST) + b2_ref[...]))

    h3 = jnp.dot(h_sc[...], w3_ref[...], preferred_element_type=F32, precision=HIGHEST)
    t_col = (i * tt + lax.broadcasted_iota(I32, (tt, 1), 0)).astype(F32) / float(seq - 1)
    win = jnp.exp(-t_col * jnp.abs(dec_ref[...]))
    out = h3 * win
    row = i * tt + lax.broadcasted_iota(I32, out.shape, 0)
    out = jnp.where((row == 0) & (part >= nparts // 2), 0.0, out)
    _store_fft_layout(out, o_ref, scr)


def _hyena_filters(seq, w1, b1, f1, w2, b2, f2, w3, decay):
    ffn = w1.shape[1]
    c = HY_WIDTH
    nparts = w3.shape[1] // c
    tt = SUBLANES * FFT_N1
    band = jnp.linspace(1e-4, HY_BANDS - 1, HY_BANDS, dtype=F32)
    freq = jnp.zeros((1, LANES), F32).at[0, 1:1 + HY_BANDS].set(band).at[0, 1 + HY_BANDS:1 + 2 * HY_BANDS].set(band)
    w1p = jnp.zeros((LANES, ffn), F32).at[:w1.shape[0]].set(w1)
    full = lambda shape: pl.BlockSpec(shape, lambda i, p: (0,) * len(shape))
    return pl.pallas_call(
        functools.partial(_filter_kernel, tt=tt, seq=seq, nparts=nparts),
        grid=(seq // tt, nparts),
        in_specs=[full((1, LANES)), full((LANES, ffn)), full((1, ffn)), full((1, ffn)),
                  full((ffn, ffn)), full((1, ffn)), full((1, ffn)),
                  pl.BlockSpec((ffn, c), lambda i, p: (0, p)), pl.BlockSpec((1, c), lambda i, p: (0, p))],
        out_specs=pl.BlockSpec((1, SUBLANES, FFT_N1 * c), lambda i, p: (p, i, 0)),
        out_shape=jax.ShapeDtypeStruct((nparts, seq // FFT_N1, FFT_N1 * c), F32),
        scratch_shapes=[pltpu.VMEM((c // LANES, tt, LANES), F32), pltpu.VMEM((tt, ffn), F32)],
        compiler_params=_cparams(2),
        name="hyena_filters",
    )(freq, w1p, b1.reshape(1, ffn), f1.reshape(1, ffn), w2, b2.reshape(1, ffn), f2.reshape(1, ffn),
      w3, decay.reshape(1, -1))


def _fft_tables(seq):
    n2h = seq // FFT_N1
    n2 = 2 * n2h
    n = FFT_N1 * n2
    k2n = n2h + 1
    k2p = _round_up(k2n, BF16_ROWS)
    kk = np.arange(k2p)[:, None]
    nn = np.arange(n2h)[None, :]
    ang = 2.0 * np.pi * ((kk * nn) % n2) / n2
    valid = (kk < k2n).astype(np.float64)
    fa = np.concatenate([np.cos(ang) * valid, -np.sin(ang) * valid], axis=0)
    wk = np.where((kk == 0) | (kk == n2h), 1.0, 2.0) * valid
    gc = np.concatenate([(np.cos(ang) * wk).T, (-np.sin(ang) * wk).T], axis=1)
    n1 = np.arange(FFT_N1)[None, :]
    tw_ang = 2.0 * np.pi * ((kk * n1) % n) / n
    twr = (np.cos(tw_ang) * valid).reshape(k2p, 1, FFT_N1)
    twi = (-np.sin(tw_ang) * valid).reshape(k2p, 1, FFT_N1)
    k1 = np.arange(FFT_N1)[:, None]
    base = 2.0 * np.pi * ((k1 * n1) % FFT_N1) / FFT_N1
    return dict(n2h=n2h, k2p=k2p, n=n,
                fa=jnp.asarray(fa, BF16), gc=jnp.asarray(gc, BF16),
                twr=jnp.asarray(twr, F32), twi=jnp.asarray(twi, F32),
                cr=jnp.asarray(np.cos(base), F32), ci=jnp.asarray(-np.sin(base), F32))


def _pack_pair(re, im):
    hi = lax.bitcast_convert_type(re.astype(BF16).astype(F32), U32)
    lo = lax.bitcast_convert_type(im.astype(BF16).astype(F32), U32)
    return hi | (lo >> 16)


def _unpack_pair(w):
    re = lax.bitcast_convert_type(w & jnp.uint32(0xFFFF0000), F32)
    im = lax.bitcast_convert_type(w << 16, F32)
    return re, im


def _ffta_kernel(x_ref, fa_ref, o_ref, *, k2p, c):
    for j in range(SUBLANES):
        xs = x_ref[0, :, j * c:(j + 1) * c].astype(BF16)
        res = jnp.dot(fa_ref[...], xs, preferred_element_type=F32)
        w = _pack_pair(res[:k2p], res[k2p:])
        for ct in range(c // LANES):
            o_ref[0, 0, ct, pl.ds(j, k2p, stride=SUBLANES), :] = w[:, ct * LANES:(ct + 1) * LANES]


def _fft_stage_a(xf, fa, c):
    bx, n2h, _ = xf.shape
    k2p = fa.shape[0] // 2
    ngrp = FFT_N1 // SUBLANES
    nct = c // LANES
    return pl.pallas_call(
        functools.partial(_ffta_kernel, k2p=k2p, c=c),
        grid=(bx, ngrp),
        in_specs=[pl.BlockSpec((1, n2h, SUBLANES * c), lambda b, g: (b, 0, g)),
                  pl.BlockSpec((2 * k2p, n2h), lambda b, g: (0, 0))],
        out_specs=pl.BlockSpec((1, 1, nct, k2p * SUBLANES, LANES), lambda b, g: (b, g, 0, 0, 0)),
        out_shape=jax.ShapeDtypeStruct((bx, ngrp, nct, k2p * SUBLANES, LANES), U32),
        compiler_params=_cparams(2),
        name="fft_stage_a",
    )(xf, fa)


def _load_stage_b(a_ref, b):
    ngrp, nct = a_ref.shape[1], a_ref.shape[2]
    return jnp.concatenate([jnp.concatenate([a_ref[b, g, ct] for ct in range(nct)], axis=1)
                            for g in range(ngrp)], axis=0)


def _stage_b_matrix(twr_ref, twi_ref, cr_ref, ci_ref):
    tr = twr_ref[0]
    ti = twi_ref[0]
    cr = cr_ref[...]
    ci = ci_ref[...]
    mr = cr * tr - ci * ti
    mi = ci * tr + cr * ti
    return mr, mi


def _fftb_filter_kernel(a_ref, twr_ref, twi_ref, cr_ref, ci_ref, o_ref, *, scale):
    mr, mi = _stage_b_matrix(twr_ref, twi_ref, cr_ref, ci_ref)
    fb = jnp.concatenate([jnp.concatenate([mr, -mi], axis=1),
                          jnp.concatenate([mi, mr], axis=1)], axis=0).astype(BF16)
    nparts = a_ref.shape[0]
    c = a_ref.shape[2] * LANES
    hs = []
    for p in range(nparts):
        ar, ai = _unpack_pair(_load_stage_b(a_ref, p))
        a = jnp.concatenate([ar, ai], axis=0).astype(BF16)
        hs.append(jnp.dot(fb, a, preferred_element_type=F32))
    for order in range(nparts // 2):
        fwd, bwd = hs[order], hs[nparts // 2 + order]
        o_ref[0, 0, :, order * c:(order + 1) * c] = (fwd[:FFT_N1] + bwd[:FFT_N1]) * scale
        o_ref[0, 1, :, order * c:(order + 1) * c] = (fwd[FFT_N1:] - bwd[FFT_N1:]) * scale


def _fft_stage_b_filter(ap, tabs, scale):
    k2p = tabs["k2p"]
    nparts, ngrp, nct = ap.shape[:3]
    c = nct * LANES
    return pl.pallas_call(
        functools.partial(_fftb_filter_kernel, scale=scale),
        grid=(k2p,),
        in_specs=[pl.BlockSpec((nparts, ngrp, nct, SUBLANES, LANES), lambda k: (0, 0, 0, k, 0)),
                  pl.BlockSpec((1, 1, FFT_N1), lambda k: (k, 0, 0)),
                  pl.BlockSpec((1, 1, FFT_N1), lambda k: (k, 0, 0)),
                  pl.BlockSpec((FFT_N1, FFT_N1), lambda k: (0, 0)),
                  pl.BlockSpec((FFT_N1, FFT_N1), lambda k: (0, 0))],
        out_specs=pl.BlockSpec((1, 2, FFT_N1, nparts // 2 * c), lambda k: (k, 0, 0, 0)),
        out_shape=jax.ShapeDtypeStruct((k2p, 2, FFT_N1, nparts // 2 * c), F32),
        compiler_params=_cparams(1),
        name="fft_stage_b_filter",
    )(ap, tabs["twr"], tabs["twi"], tabs["cr"], tabs["ci"])


def _fftb_conv_kernel(a_ref, g_ref, twr_ref, twi_ref, cr_ref, ci_ref, o_ref):
    mr, mi = _stage_b_matrix(twr_ref, twi_ref, cr_ref, ci_ref)
    fb = jnp.concatenate([jnp.concatenate([mr, -mi], axis=1),
                          jnp.concatenate([mi, mr], axis=1)], axis=0).astype(BF16)
    mrt, mit = mr.T, mi.T
    fbt = jnp.concatenate([jnp.concatenate([mrt, mit], axis=1),
                           jnp.concatenate([-mit, mrt], axis=1)], axis=0).astype(BF16)
    gr = g_ref[0, 0]
    gi = g_ref[0, 1]
    ngrp, nct = a_ref.shape[1], a_ref.shape[2]
    for b in range(a_ref.shape[0]):
        ar, ai = _unpack_pair(_load_stage_b(a_ref, b))
        a = jnp.concatenate([ar, ai], axis=0).astype(BF16)
        xh = jnp.dot(fb, a, preferred_element_type=F32)
        xr, xi = xh[:FFT_N1], xh[FFT_N1:]
        y = jnp.concatenate([xr * gr - xi * gi, xr * gi + xi * gr], axis=0).astype(BF16)
        cc = jnp.dot(fbt, y, preferred_element_type=F32)
        w = _pack_pair(cc[:FFT_N1], cc[FFT_N1:])
        for g in range(ngrp):
            for ct in range(nct):
                o_ref[b, g, ct] = w[g * SUBLANES:(g + 1) * SUBLANES, ct * LANES:(ct + 1) * LANES]


def _fft_stage_b_conv(ap, g, order, tabs):
    bsz, ngrp, nct = ap.shape[:3]
    k2p = tabs["k2p"]
    c = nct * LANES
    blk = pl.BlockSpec((bsz, ngrp, nct, SUBLANES, LANES), lambda k: (0, 0, 0, k, 0))
    return pl.pallas_call(
        _fftb_conv_kernel,
        grid=(k2p,),
        in_specs=[blk,
                  pl.BlockSpec((1, 2, FFT_N1, c), lambda k: (k, 0, 0, order)),
                  pl.BlockSpec((1, 1, FFT_N1), lambda k: (k, 0, 0)),
                  pl.BlockSpec((1, 1, FFT_N1), lambda k: (k, 0, 0)),
                  pl.BlockSpec((FFT_N1, FFT_N1), lambda k: (0, 0)),
                  pl.BlockSpec((FFT_N1, FFT_N1), lambda k: (0, 0))],
        out_specs=blk,
        out_shape=jax.ShapeDtypeStruct(ap.shape, U32),
        compiler_params=_cparams(1),
        name="fft_stage_b_conv",
    )(ap, g, tabs["twr"], tabs["twi"], tabs["cr"], tabs["ci"])


def _fftc_kernel(c_ref, gc_ref, u_ref, gate_ref, d_ref, o_ref, *, k2p, c):
    for j in range(SUBLANES):
        w = jnp.concatenate([c_ref[0, 0, ct, pl.ds(j, k2p, stride=SUBLANES), :] for ct in range(c // LANES)],
                            axis=1)
        re, im = _unpack_pair(w)
        cm = jnp.concatenate([re, im], axis=0).astype(BF16)
        y = jnp.dot(gc_ref[...], cm, preferred_element_type=F32)
        sl = slice(j * c, (j + 1) * c)
        o_ref[0, :, sl] = gate_ref[0, :, sl] * (y + d_ref[...] * u_ref[0, :, sl])


def _fft_stage_c(cp, gc, uf, gatef, d_row):
    bsz, ngrp, nct = cp.shape[:3]
    n2h = uf.shape[1]
    c = nct * LANES
    k2p = gc.shape[1] // 2
    fblk = pl.BlockSpec((1, n2h, SUBLANES * c), lambda b, g: (b, 0, g))
    return pl.pallas_call(
        functools.partial(_fftc_kernel, k2p=k2p, c=c),
        grid=(bsz, ngrp),
        in_specs=[pl.BlockSpec((1, 1, nct, k2p * SUBLANES, LANES), lambda b, g: (b, g, 0, 0, 0)),
                  pl.BlockSpec((n2h, 2 * k2p), lambda b, g: (0, 0)),
                  fblk, fblk,
                  pl.BlockSpec((1, c), lambda b, g: (0, 0))],
        out_specs=fblk,
        out_shape=jax.ShapeDtypeStruct(uf.shape, F32),
        compiler_params=_cparams(2),
        name="fft_stage_c",
    )(cp, gc, uf, gatef, d_row)


def _hyena_mix(hv, hx1, hx2, filt, skip, seq):
    c = skip.shape[1]
    tabs = _fft_tables(seq)
    g = _fft_stage_b_filter(_fft_stage_a(filt, tabs["fa"], c), tabs, 1.0 / tabs["n"])

    def conv(u, gate, order):
        cp = _fft_stage_b_conv(_fft_stage_a(u, tabs["fa"], c), g, order, tabs)
        return _fft_stage_c(cp, tabs["gc"], u, gate, skip[order].reshape(1, c))

    return conv(conv(hv, hx1, 0), hx2, 1)


def _qkprep_kernel(q_ref, k_ref, v_ref, qn_ref, kn_ref, oqt_ref, ok_ref, ovt_ref, *, tt, seq):
    i = pl.program_id(0)
    t = (i * tt + lax.broadcasted_iota(I32, (tt, HEAD_DIM), 0)) % seq
    lane = lax.broadcasted_iota(I32, (tt, HEAD_DIM), 1)
    j = lane % (HEAD_DIM // 2)
    quarter = HEAD_DIM // 4
    f = (j % quarter).astype(F32)
    inv = jnp.exp(f * (-2.0 * math.log(ROPE_THETA) / (HEAD_DIM // 2)))
    pos = jnp.where(j < quarter, t // GRID_W, t % GRID_W).astype(F32)
    ang = pos * inv
    cosv = jnp.cos(ang)
    sinv = jnp.sin(ang)
    sin_signed = jnp.where(lane < HEAD_DIM // 2, -sinv, sinv)

    def norm_rope(x, g, scale):
        ms = jnp.mean(x * x, axis=-1, keepdims=True)
        xn = x * lax.rsqrt(ms + NORM_EPS) * g
        return (xn * cosv + pltpu.roll(xn, HEAD_DIM // 2, axis=1) * sin_signed) * scale

    qg = qn_ref[...]
    kg = kn_ref[...]
    q_scale = LOG2_E * HEAD_DIM ** -0.5
    for h in range(N_HEADS):
        sl = slice(h * HEAD_DIM, (h + 1) * HEAD_DIM)
        oqt_ref[h] = norm_rope(q_ref[:, sl], qg, q_scale).T.astype(BF16)
    for h in range(N_KV_HEADS):
        sl = slice(h * HEAD_DIM, (h + 1) * HEAD_DIM)
        ok_ref[:, sl] = norm_rope(k_ref[:, sl], kg, 1.0).astype(BF16)
        ovt_ref[h] = v_ref[:, sl].T.astype(BF16)


def _qk_prep(proj, qn, kn, seq):
    n = proj.shape[0]
    tt = min(512, seq)
    qw = N_HEADS * HEAD_DIM
    kw = N_KV_HEADS * HEAD_DIM
    q_off = (3 * HY_WIDTH) // qw
    k_off = (3 * HY_WIDTH + qw) // kw
    return pl.pallas_call(
        functools.partial(_qkprep_kernel, tt=tt, seq=seq),
        grid=(n // tt,),
        in_specs=[pl.BlockSpec((tt, qw), lambda i: (i, q_off)),
                  pl.BlockSpec((tt, kw), lambda i: (i, k_off)),
                  pl.BlockSpec((tt, kw), lambda i: (i, k_off + 1)),
                  pl.BlockSpec((1, HEAD_DIM), lambda i: (0, 0)),
                  pl.BlockSpec((1, HEAD_DIM), lambda i: (0, 0))],
        out_specs=[pl.BlockSpec((N_HEADS, HEAD_DIM, tt), lambda i: (0, 0, i)),
                   pl.BlockSpec((tt, kw), lambda i: (i, 0)),
                   pl.BlockSpec((N_KV_HEADS, HEAD_DIM, tt), lambda i: (0, 0, i))],
        out_shape=[jax.ShapeDtypeStruct((N_HEADS, HEAD_DIM, n), BF16),
                   jax.ShapeDtypeStruct((n, kw), BF16),
                   jax.ShapeDtypeStruct((N_KV_HEADS, HEAD_DIM, n), BF16)],
        compiler_params=_cparams(1),
        name="qk_prep",
    )(proj, proj, proj, qn, kn)


def _flash_kernel(qt_ref, k_ref, vt_ref, o_ref, acc_sc, s_sc, *, tq, tk, seq, unroll):
    nq = Q_PER_KV * tq
    nk = seq // tk
    qt = jnp.concatenate([qt_ref[g] for g in range(Q_PER_KV)], axis=1)
    acc_sc[...] = jnp.zeros(acc_sc.shape, F32)

    def scores(c):
        start = pl.multiple_of(c * tk, tk)
        return jnp.dot(k_ref[pl.ds(start, tk), :], qt, preferred_element_type=F32)

    s_sc[0] = scores(0)

    def body(i, carry):
        m, l = carry
        for u in range(unroll):
            c = i * unroll + u
            s_sc[(u + 1) % 2] = scores(jnp.minimum(c + 1, nk - 1))
            s = s_sc[u % 2]
            start = pl.multiple_of(c * tk, tk)
            vc = vt_ref[0, :, pl.ds(start, tk)]
            mc = jnp.max(s, axis=0, keepdims=True)
            p = jnp.exp2(s - mc)
            lc = jnp.sum(p, axis=0, keepdims=True)
            oc = jnp.dot(vc, p.astype(BF16), preferred_element_type=F32)
            m_new = jnp.maximum(m, mc)
            a = jnp.exp2(m - m_new)
            bc = jnp.exp2(mc - m_new)
            l = l * a + lc * bc
            acc_sc[...] = acc_sc[...] * a + oc * bc
            m = m_new
        return m, l

    m0 = jnp.full((1, nq), -jnp.inf, F32)
    l0 = jnp.zeros((1, nq), F32)
    _, l = lax.fori_loop(0, nk // unroll, body, (m0, l0))
    o = acc_sc[...] / l
    for g in range(Q_PER_KV):
        o_ref[:, g * HEAD_DIM:(g + 1) * HEAD_DIM] = o[:, g * tq:(g + 1) * tq].T


def _flash_attention(qt, k, vt, bsz, seq):
    n = bsz * seq
    tq = min(256, seq)
    tk = min(1024, seq // 2)
    unroll = 2 if seq // tk >= 2 else 1
    gw = Q_PER_KV * HEAD_DIM
    nqb = seq // tq
    return pl.pallas_call(
        functools.partial(_flash_kernel, tq=tq, tk=tk, seq=seq, unroll=unroll),
        grid=(bsz, N_KV_HEADS, nqb),
        in_specs=[pl.BlockSpec((Q_PER_KV, HEAD_DIM, tq), lambda b, h, i: (h, 0, b * nqb + i)),
                  pl.BlockSpec((seq, HEAD_DIM), lambda b, h, i: (b, h)),
                  pl.BlockSpec((1, HEAD_DIM, seq), lambda b, h, i: (h, 0, b))],
        out_specs=pl.BlockSpec((tq, gw), lambda b, h, i: (b * nqb + i, h)),
        out_shape=jax.ShapeDtypeStruct((n, N_HEADS * HEAD_DIM), F32),
        scratch_shapes=[pltpu.VMEM((HEAD_DIM, Q_PER_KV * tq), F32),
                        pltpu.VMEM((2, tk, Q_PER_KV * tq), F32)],
        compiler_params=_cparams(3),
        name="flash_attention",
    )(qt, k, vt)


def _layer_norm(y, g, b):
    mu = jnp.mean(y, axis=-1, keepdims=True)
    yc = y - mu
    var = jnp.mean(yc * yc, axis=-1, keepdims=True)
    return yc * lax.rsqrt(var + NORM_EPS) * g + b


def _merge_kernel(yh_ref, ya_ref, x_ref, gh_ref, ga_ref, wo_ref, lg_ref, lb_ref, wr_ref, br_ref,
                  x1_ref, aff_ref):
    hw = yh_ref.shape[1]
    yh = yh_ref[...]
    ya = ya_ref[...]
    mh = yh * lax.rsqrt(jnp.mean(yh * yh, axis=-1, keepdims=True) + NORM_EPS) * gh_ref[...]
    ma = ya * lax.rsqrt(jnp.mean(ya * ya, axis=-1, keepdims=True) + NORM_EPS) * ga_ref[...]
    y = (jnp.dot(mh.astype(BF16), wo_ref[:hw, :], preferred_element_type=F32)
         + jnp.dot(ma.astype(BF16), wo_ref[hw:, :], preferred_element_type=F32)
         + DN_ALPHA * x_ref[...])
    x1 = _layer_norm(y, lg_ref[...], lb_ref[...])
    x1_ref[...] = x1
    x_hi = x1.astype(BF16)
    x_lo = (x1 - x_hi.astype(F32)).astype(BF16)
    both = jnp.dot(x_hi, wr_ref[...], preferred_element_type=F32)
    logits = (both[:, :LANES] + both[:, LANES:]
              + jnp.dot(x_lo, wr_ref[:, :LANES], preferred_element_type=F32) + br_ref[...])
    e = jnp.exp(logits - jnp.max(logits, axis=-1, keepdims=True))
    aff_ref[...] = e / jnp.sum(e, axis=-1, keepdims=True)


def _merge(y_hy, y_att, x2, g_hy, g_attn, wo_bf, ln_g, ln_b, wr_pad, br_pad):
    n, d = x2.shape
    hw = y_hy.shape[1]
    tm = min(256, n)
    row = lambda w: pl.BlockSpec((1, w), lambda i: (0, 0))
    return pl.pallas_call(
        _merge_kernel,
        grid=(n // tm,),
        in_specs=[pl.BlockSpec((tm, hw), lambda i: (i, 0)),
                  pl.BlockSpec((tm, d - hw), lambda i: (i, 0)),
                  pl.BlockSpec((tm, d), lambda i: (i, 0)),
                  row(hw), row(d - hw),
                  pl.BlockSpec((d, d), lambda i: (0, 0)),
                  row(d), row(d),
                  pl.BlockSpec((d, 2 * LANES), lambda i: (0, 0)),
                  row(LANES)],
        out_specs=[pl.BlockSpec((tm, d), lambda i: (i, 0)),
                   pl.BlockSpec((tm, LANES), lambda i: (i, 0))],
        out_shape=[jax.ShapeDtypeStruct((n, d), F32), jax.ShapeDtypeStruct((n, LANES), F32)],
        compiler_params=_cparams(1),
        name="merge_ln_router",
    )(y_hy, y_att, x2, g_hy, g_attn, wo_bf, ln_g, ln_b, wr_pad, br_pad)


def _select_kernel(a_ref, idx_ref, gslot_ref, pos_ref, excl_ref, *, cap):
    a = a_ref[0]
    ne, nr, _ = a.shape
    bits = lax.bitcast_convert_type(a, I32)

    def count(mask):
        return jnp.sum(jnp.sum(mask.astype(F32), axis=2, keepdims=True), axis=1, keepdims=True)

    thr = jnp.zeros((ne, 1, 1), I32)
    for bit in range(30, -1, -1):
        cand = thr | (1 << bit)
        thr = jnp.where(count(bits >= cand) >= cap, cand, thr)

    ii = lax.broadcasted_iota(I32, (LANES, LANES), 0)
    jj = lax.broadcasted_iota(I32, (LANES, LANES), 1)
    upper = (ii <= jj).astype(BF16)
    ri = lax.broadcasted_iota(I32, (ne, nr, nr), 1)
    rj = lax.broadcasted_iota(I32, (ne, nr, nr), 2)
    lower_strict = (rj < ri).astype(BF16)

    def prefix(mask):
        mb = mask.astype(BF16).reshape(ne * nr, LANES)
        pin = jnp.dot(mb, upper, preferred_element_type=F32).reshape(ne, nr, LANES)
        tot = pin[:, :, LANES - 1:LANES]
        totb = jnp.broadcast_to(tot, (ne, nr, LANES)).astype(BF16)
        off = jnp.einsum("ers,esl->erl", lower_strict, totb, preferred_element_type=F32)
        return pin, tot, off[:, :, 0:1]

    gt = bits > thr
    eq = bits == thr
    need = cap - count(gt)
    pin_eq, _, off_eq = prefix(eq)
    eq_rank = pin_eq + off_eq - eq.astype(F32)
    chosen = gt | (eq & (eq_rank < need))
    pin, tot, off = prefix(chosen)
    chosen_f = chosen.astype(F32)
    excl = pin + off - chosen_f
    excl_ref[0] = excl
    pos_ref[0] = jnp.where(chosen, excl, -1.0)

    s_row = lax.broadcasted_iota(I32, (nr, cap), 1).astype(F32)
    r_col = lax.broadcasted_iota(I32, (nr, cap), 0).astype(F32)
    j_col = lax.broadcasted_iota(I32, (LANES, cap), 0).astype(F32)
    for e in range(ne):
        lo = off[e]
        hi = lo + tot[e]
        in_row = ((lo <= s_row) & (s_row < hi)).astype(F32)
        in_row_b = in_row.astype(BF16)
        row_of_s = jnp.sum(in_row * r_col, axis=0, keepdims=True)
        s_local = s_row[0:1] - jnp.sum(in_row * lo, axis=0, keepdims=True)
        pin_t = pin[e].T.astype(BF16)
        g = jnp.dot(pin_t, in_row_b, preferred_element_type=F32)
        lane_of_s = jnp.sum((g <= s_local).astype(F32), axis=0, keepdims=True)
        idx_ref[0, e:e + 1, :] = (row_of_s * LANES + lane_of_s).astype(I32)
        rem = a[e]
        a_of_s = jnp.zeros((LANES, cap), F32)
        for _ in range(3):
            part = rem.astype(BF16)
            rem = rem - part.astype(F32)
            a_of_s = a_of_s + jnp.dot(part.astype(F32).T.astype(BF16), in_row_b, preferred_element_type=F32)
        gslot_ref[0, e:e + 1, :] = jnp.sum(jnp.where(j_col == lane_of_s, a_of_s, 0.0), axis=0, keepdims=True)


def _select(aff4, cap):
    bsz, ne, nr, _ = aff4.shape
    blk = pl.BlockSpec((1, ne, nr, LANES), lambda b: (b, 0, 0, 0))
    slot = pl.BlockSpec((1, ne, cap), lambda b: (b, 0, 0))
    return pl.pallas_call(
        functools.partial(_select_kernel, cap=cap),
        grid=(bsz,),
        in_specs=[blk],
        out_specs=[slot, slot, blk, blk],
        out_shape=[jax.ShapeDtypeStruct((bsz, ne, cap), I32),
                   jax.ShapeDtypeStruct((bsz, ne, cap), F32),
                   jax.ShapeDtypeStruct(aff4.shape, F32),
                   jax.ShapeDtypeStruct(aff4.shape, F32)],
        compiler_params=_cparams(1),
        name="expert_choice_select",
    )(aff4)


def _expert_kernel(idx_ref, idxn_ref, x_hbm, g_ref, wgu_ref, wd_ref, o_ref, xg_ref, sem,
                   *, tc, seq, ff, nj, bsz, nsteps):
    s = pl.program_id(0)
    slot = s % 2

    def issue(ids_ref, step, to_slot):
        base = ((step // nj) % bsz) * seq
        for r in range(tc):
            pltpu.make_async_copy(x_hbm.at[pl.ds(base + ids_ref[0, 0, r], 1)],
                                  xg_ref.at[to_slot, pl.ds(r, 1)], sem.at[to_slot]).start()

    def wait_rows(of_slot):
        pltpu.make_async_copy(x_hbm.at[pl.ds(0, tc)], xg_ref.at[of_slot], sem.at[of_slot]).wait()

    @pl.when(s == 0)
    def _():
        issue(idx_ref, s, 0)

    issue(idxn_ref, jnp.minimum(s + 1, nsteps - 1), 1 - slot)
    wait_rows(slot)
    xb = xg_ref[slot].astype(BF16)
    gu = jnp.dot(xb, wgu_ref[0], preferred_element_type=F32)
    gate = gu[:, :ff]
    h = gate * jax.nn.sigmoid(gate) * gu[:, ff:]
    y = jnp.dot(h.astype(BF16), wd_ref[0], preferred_element_type=F32)
    o_ref[0, 0] = (y * g_ref[0]).astype(BF16)

    @pl.when(s == nsteps - 1)
    def _():
        wait_rows(1 - slot)


def _experts(idx, gslot, x1, wgu_bf, wd_bf, bsz, seq):
    _, ne, cap = idx.shape
    d = x1.shape[1]
    ff = wd_bf.shape[1]
    tc = min(256, cap)
    nj = cap // tc
    nsteps = ne * bsz * nj
    idx_s = idx.transpose(1, 0, 2).reshape(nsteps, 1, tc)
    g_s = gslot.transpose(1, 0, 2).reshape(nsteps, tc, 1)
    out_map = lambda s: ((s // nj) % bsz, s // (bsz * nj), s % nj, 0)
    return pl.pallas_call(
        functools.partial(_expert_kernel, tc=tc, seq=seq, ff=ff, nj=nj, bsz=bsz, nsteps=nsteps),
        grid=(nsteps,),
        in_specs=[pl.BlockSpec((1, 1, tc), lambda s: (s, 0, 0), memory_space=pltpu.SMEM),
                  pl.BlockSpec((1, 1, tc), lambda s: (jnp.minimum(s + 1, nsteps - 1), 0, 0),
                               memory_space=pltpu.SMEM),
                  pl.BlockSpec(memory_space=pl.ANY),
                  pl.BlockSpec((1, tc, 1), lambda s: (s, 0, 0)),
                  pl.BlockSpec((1, d, 2 * ff), lambda s: (s // (bsz * nj), 0, 0)),
                  pl.BlockSpec((1, ff, d), lambda s: (s // (bsz * nj), 0, 0))],
        out_specs=pl.BlockSpec((1, 1, tc, d), out_map),
        out_shape=jax.ShapeDtypeStruct((bsz, ne, cap, d), BF16),
        scratch_shapes=[pltpu.VMEM((2, tc, d), F32), pltpu.SemaphoreType.DMA((2,))],
        compiler_params=_cparams(1),
        name="expert_ffn",
    )(idx_s, idx_s, x1, g_s, wgu_bf, wd_bf)


def _combine_kernel(w0_ref, y_hbm, x1_ref, pos_ref, lg_ref, lb_ref, o_ref, ybuf, xbuf, acc_ref, sem, xsem,
                    *, tb_tokens, win, cap, ne, ntb):
    b = pl.program_id(0)
    tb = pl.program_id(1)
    base = (b * (ntb + 1) + tb) * ne
    starts = []
    for e in range(ne):
        sa = jnp.minimum((w0_ref[base + e] // BF16_ROWS) * BF16_ROWS, cap - win)
        sa = pl.multiple_of(sa, BF16_ROWS)
        starts.append(sa)
        pltpu.make_async_copy(y_hbm.at[b, e, pl.ds(sa, win)], ybuf.at[pl.ds(e * win, win)], sem.at[e]).start()
    lane = lax.broadcasted_iota(I32, (tb_tokens, win), 1).astype(F32)
    onehots = [(pos_ref[0, :, e:e + 1] - starts[e].astype(F32) == lane).astype(BF16) for e in range(ne)]
    p = jnp.concatenate(onehots, axis=1)
    for e in range(ne):
        pltpu.make_async_copy(y_hbm.at[b, e, pl.ds(starts[e], win)], ybuf.at[pl.ds(e * win, win)],
                              sem.at[e]).wait()
    acc_ref[...] = DN_ALPHA * x1_ref[...] + jnp.dot(p, ybuf[...], preferred_element_type=F32)

    for e in range(ne):
        first_end = starts[e] + win
        n_extra = jnp.maximum(w0_ref[base + ne + e] - first_end + win - 1, 0) // win

        def extra(w, carry, e=e, first_end=first_end):
            nominal = first_end + w * win
            st = pl.multiple_of(jnp.minimum(nominal, cap - win), BF16_ROWS)
            cp = pltpu.make_async_copy(y_hbm.at[b, e, pl.ds(st, win)], xbuf, xsem)
            cp.start()
            cp.wait()
            pe = pos_ref[0, :, e:e + 1]
            oh = ((pe - st.astype(F32) == lane) & (pe >= nominal.astype(F32))).astype(BF16)
            acc_ref[...] += jnp.dot(oh, xbuf[...], preferred_element_type=F32)
            return carry

        lax.fori_loop(0, n_extra, extra, 0)
    o_ref[...] = _layer_norm(acc_ref[...], lg_ref[...], lb_ref[...])


def _combine(y, x1, pos_t, w0, ln_g, ln_b, bsz, seq, tbt):
    _, ne, cap, d = y.shape
    win = min(LANES, cap)
    ntb = seq // tbt
    grid_spec = pltpu.PrefetchScalarGridSpec(
        num_scalar_prefetch=1,
        grid=(bsz, ntb),
        in_specs=[pl.BlockSpec(memory_space=pl.ANY),
                  pl.BlockSpec((tbt, d), lambda b, t, w: (b * ntb + t, 0)),
                  pl.BlockSpec((1, tbt, ne), lambda b, t, w: (b, t, 0)),
                  pl.BlockSpec((1, d), lambda b, t, w: (0, 0)),
                  pl.BlockSpec((1, d), lambda b, t, w: (0, 0))],
        out_specs=pl.BlockSpec((tbt, d), lambda b, t, w: (b * ntb + t, 0)),
        scratch_shapes=[pltpu.VMEM((ne * win, d), BF16), pltpu.VMEM((win, d), BF16),
                        pltpu.VMEM((tbt, d), F32),
                        pltpu.SemaphoreType.DMA((ne,)), pltpu.SemaphoreType.DMA(())],
    )
    return pl.pallas_call(
        functools.partial(_combine_kernel, tb_tokens=tbt, win=win, cap=cap, ne=ne, ntb=ntb),
        grid_spec=grid_spec,
        out_shape=jax.ShapeDtypeStruct((bsz * seq, d), F32),
        compiler_params=_cparams(2),
        name="moe_combine_ln",
    )(w0, y, x1, pos_t, ln_g, ln_b)


def _deinterleave(width):
    idx = np.arange(width).reshape(-1, HEAD_DIM // 2, 2)
    return np.concatenate([idx[:, :, 0], idx[:, :, 1]], axis=1).reshape(-1)


def kernel(x, w_in, b_in, hy_conv_w, hy_conv_b, hy_ffn_w1, hy_ffn_b1, hy_sin_f1, hy_ffn_w2, hy_ffn_b2, hy_sin_f2, hy_ffn_w3, hy_decay, hy_skip, q_norm, k_norm, g_hy, g_attn, w_out, ln1_g, ln1_b, w_router, b_router, w_gate, w_up, w_down, ln2_g, ln2_b):
    bsz, seq, d = x.shape
    n = bsz * seq
    x2 = x.reshape(n, d)
    s1 = 3 * HY_WIDTH
    qw = N_HEADS * HEAD_DIM
    kw = N_KV_HEADS * HEAD_DIM
    perm = np.concatenate([np.arange(s1), s1 + _deinterleave(qw), s1 + qw + _deinterleave(kw),
                           np.arange(s1 + qw + kw, w_in.shape[1])])
    w_in_bf = w_in[:, perm].astype(BF16)
    b_in_p = b_in[perm].reshape(1, -1)
    hd_perm = _deinterleave(HEAD_DIM)

    proj = _in_projection(x2, w_in_bf, b_in_p)

    conv_b = hy_conv_b.reshape(1, -1)
    hv, hx1, hx2 = [_short_conv(proj, hy_conv_w, conv_b, part, bsz, seq) for part in range(3)]
    filt = _hyena_filters(seq, hy_ffn_w1, hy_ffn_b1, hy_sin_f1, hy_ffn_w2, hy_ffn_b2, hy_sin_f2,
                          hy_ffn_w3, hy_decay)
    y_hy = _hyena_mix(hv, hx1, hx2, filt, hy_skip, seq).reshape(n, HY_WIDTH)

    qt, kb, vt = _qk_prep(proj, q_norm[hd_perm].reshape(1, HEAD_DIM), k_norm[hd_perm].reshape(1, HEAD_DIM), seq)
    y_att = _flash_attention(qt, kb, vt, bsz, seq)

    ne = w_router.shape[1]
    wr_hi = w_router.astype(BF16)
    wr_lo = (w_router - wr_hi.astype(F32)).astype(BF16)
    wr_pair = (jnp.zeros((d, 2 * LANES), BF16).at[:, :ne].set(wr_hi).at[:, LANES:LANES + ne].set(wr_lo))
    br_pad = jnp.full((1, LANES), -1e30, F32).at[0, :ne].set(b_router)
    x1, aff = _merge(y_hy, y_att, x2, g_hy.reshape(1, -1), g_attn.reshape(1, -1),
                     w_out.astype(BF16), ln1_g.reshape(1, d), ln1_b.reshape(1, d), wr_pair, br_pad)

    cap = CAPACITY_FACTOR * seq // ne
    aff4 = aff[:, :ne].reshape(bsz, seq, ne).transpose(0, 2, 1).reshape(bsz, ne, seq // LANES, LANES)
    idx, gslot, pos, excl = _select(aff4, cap)
    wgu_bf = jnp.concatenate([w_gate, w_up], axis=2).astype(BF16)
    y = _experts(idx, gslot, x1, wgu_bf, w_down.astype(BF16), bsz, seq)
    pos_t = pos.reshape(bsz, ne, seq).transpose(0, 2, 1)
    tbt = min(256, cap)
    before = excl.reshape(bsz, ne, seq // tbt, tbt)[:, :, :, 0].transpose(0, 2, 1)
    w0 = jnp.concatenate([before, jnp.full((bsz, 1, ne), cap, F32)], axis=1).reshape(-1).astype(I32)
    out = _combine(y, x1, pos_t, w0, ln2_g.reshape(1, d), ln2_b.reshape(1, d), bsz, seq, tbt)
    return out.reshape(bsz, seq, d)
```

```python
import functools
import math

import numpy as np
import jax
import jax.numpy as jnp
from jax import lax
from jax.experimental import pallas as pl
from jax.experimental.pallas import tpu as pltpu

F32 = jnp.float32
BF16 = jnp.bfloat16
I32 = jnp.int32
U32 = jnp.uint32
HIGHEST = lax.Precision.HIGHEST

HY_WIDTH = 1024
HY_BANDS = 16
HEAD_DIM = 128
N_HEADS = 8
N_KV_HEADS = 2
Q_PER_KV = N_HEADS // N_KV_HEADS
ROPE_THETA = 10000.0
GRID_W = 64
N_EXPERTS = 16
CAPACITY_FACTOR = 2
NORM_EPS = 1e-6
DN_ALPHA = 2.0 ** 0.25
LOG2_E = math.log2(math.e)
V_ROWS = HEAD_DIM + 16

LANES = 128
SUBLANES = 8
BF16_ROWS = 16
FFT_N1 = 128
VMEM_LIMIT = 56 * 1024 * 1024


def _cparams(n_axes, vmem=VMEM_LIMIT):
    return pltpu.CompilerParams(dimension_semantics=("arbitrary",) * n_axes, vmem_limit_bytes=vmem)


def _round_up(a, m):
    return (a + m - 1) // m * m


def _inproj_kernel(x_ref, w_ref, b_ref, o_ref, xb_ref):
    @pl.when(pl.program_id(1) == 0)
    def _():
        xb_ref[...] = x_ref[...].astype(BF16)

    o_ref[...] = jnp.dot(xb_ref[...], w_ref[...], preferred_element_type=F32) + b_ref[...]


def _in_projection(x2, w_bf, b_row):
    n, d = x2.shape
    wdt = w_bf.shape[1]
    tm = min(1024, n)
    tn = wdt // 3
    return pl.pallas_call(
        _inproj_kernel,
        grid=(n // tm, wdt // tn),
        in_specs=[pl.BlockSpec((tm, d), lambda i, j: (i, 0)),
                  pl.BlockSpec((d, tn), lambda i, j: (0, j)),
                  pl.BlockSpec((1, tn), lambda i, j: (0, j))],
        out_specs=pl.BlockSpec((tm, tn), lambda i, j: (i, j)),
        out_shape=jax.ShapeDtypeStruct((n, wdt), F32),
        scratch_shapes=[pltpu.VMEM((tm, d), BF16)],
        compiler_params=_cparams(2),
        name="in_projection",
    )(x2, w_bf, b_row)


def _store_fft_layout(res, o_ref, scr):
    tt, c = res.shape
    nct = c // LANES
    for ct in range(nct):
        scr[ct] = res[:, ct * LANES:(ct + 1) * LANES]

    for n1 in range(FFT_N1):
        for ct in range(nct):
            col = n1 * c + ct * LANES
            o_ref[0, :, col:col + LANES] = scr[ct, pl.ds(n1, tt // FFT_N1, stride=FFT_N1), :]


def _shortconv_kernel(m_ref, p_ref, nx_ref, w_ref, b_ref, o_ref, scr, *, tt, seq):
    i = pl.program_id(0)
    r_in = lax.broadcasted_iota(I32, (tt, 1), 0)
    tpos = (i * tt + r_in) % seq
    cur = m_ref[...]
    up = pltpu.roll(cur, 1, axis=0)
    up = jnp.where(r_in == 0, p_ref[SUBLANES - 1:SUBLANES, :], up)
    up = jnp.where(tpos == 0, 0.0, up)
    dn = pltpu.roll(cur, tt - 1, axis=0)
    dn = jnp.where(r_in == tt - 1, nx_ref[0:1, :], dn)
    dn = jnp.where(tpos == seq - 1, 0.0, dn)
    wv = w_ref[...]
    _store_fft_layout(wv[0:1] * up + wv[1:2] * cur + wv[2:3] * dn + b_ref[...], o_ref, scr)


def _short_conv(proj, conv_w, conv_b, part, bsz, seq):
    n = proj.shape[0]
    c = HY_WIDTH
    tt = SUBLANES * FFT_N1
    nrb = n // SUBLANES
    hb = tt // SUBLANES
    tps = seq // tt
    return pl.pallas_call(
        functools.partial(_shortconv_kernel, tt=tt, seq=seq),
        grid=(n // tt,),
        in_specs=[pl.BlockSpec((tt, c), lambda i: (i, part)),
                  pl.BlockSpec((SUBLANES, c), lambda i: (jnp.maximum(i * hb - 1, 0), part)),
                  pl.BlockSpec((SUBLANES, c), lambda i: (jnp.minimum((i + 1) * hb, nrb - 1), part)),
                  pl.BlockSpec((3, c), lambda i: (0, part)),
                  pl.BlockSpec((1, c), lambda i: (0, part))],
        out_specs=pl.BlockSpec((1, SUBLANES, FFT_N1 * c), lambda i: (i // tps, i % tps, 0)),
        out_shape=jax.ShapeDtypeStruct((bsz, seq // FFT_N1, FFT_N1 * c), F32),
        scratch_shapes=[pltpu.VMEM((c // LANES, tt, LANES), F32)],
        compiler_params=_cparams(1),
        name="short_conv",
    )(proj, proj, proj, conv_w, conv_b)


def _filter_kernel(freq_ref, w1_ref, b1_ref, f1_ref, w2_ref, b2_ref, f2_ref, w3_ref, dec_ref, o_ref, scr, h_sc,
                   *, tt, seq, nparts):
    i = pl.program_id(0)
    part = pl.program_id(1)
    @pl.when(part == 0)
    def _():
        r = (i * tt + lax.broadcasted_iota(I32, (tt, LANES), 0)).astype(F32)
        lane = lax.broadcasted_iota(I32, (tt, LANES), 1)
        t = r / float(seq - 1)
        w = (2.0 * math.pi / seq) * r
        arg = freq_ref[...] * w
        emb = jnp.where(lane == 0, t,
                        jnp.where(lane <= HY_BANDS, jnp.cos(arg),
                                  jnp.where(lane <= 2 * HY_BANDS, -jnp.sin(arg), 0.0)))
        h = jnp.sin(f1_ref[...] * (jnp.dot(emb, w1_ref[...], preferred_element_type=F32, precision=HIGHEST)
                                   + b1_ref[...]))
        h_sc[...] = jnp.sin(f2_ref[...] * (jnp.dot(h, w2_ref[...], preferred_element_type=F32,
                                                   precision=HIGHEST) + b2_ref[...]))

    hf = h_sc[...]
    wf = w3_ref[...]
    h_hi = hf.astype(BF16)
    h_lo = (hf - h_hi.astype(F32)).astype(BF16)
    w_hi = wf.astype(BF16)
    w_lo = (wf - w_hi.astype(F32)).astype(BF16)
    h3 = (jnp.dot(h_hi, w_hi, preferred_element_type=F32) + jnp.dot(h_lo, w_hi, preferred_element_type=F32)
          + jnp.dot(h_hi, w_lo, preferred_element_type=F32))
    t_col = (i * tt + lax.broadcasted_iota(I32, (tt, 1), 0)).astype(F32) / float(seq - 1)
    win = jnp.exp(-t_col * jnp.abs(dec_ref[...]))
    out = h3 * win
    row = i * tt + lax.broadcasted_iota(I32, out.shape, 0)
    out = jnp.where((row == 0) & (part >= nparts // 2), 0.0, out)
    _store_fft_layout(out, o_ref, scr)


def _hyena_filters(seq, w1, b1, f1, w2, b2, f2, w3, decay):
    ffn = w1.shape[1]
    c = HY_WIDTH
    nparts = w3.shape[1] // c
    tt = SUBLANES * FFT_N1
    band = jnp.linspace(1e-4, HY_BANDS - 1, HY_BANDS, dtype=F32)
    freq = jnp.zeros((1, LANES), F32).at[0, 1:1 + HY_BANDS].set(band).at[0, 1 + HY_BANDS:1 + 2 * HY_BANDS].set(band)
    w1p = jnp.zeros((LANES, ffn), F32).at[:w1.shape[0]].set(w1)
    full = lambda shape: pl.BlockSpec(shape, lambda i, p: (0,) * len(shape))
    return pl.pallas_call(
        functools.partial(_filter_kernel, tt=tt, seq=seq, nparts=nparts),
        grid=(seq // tt, nparts),
        in_specs=[full((1, LANES)), full((LANES, ffn)), full((1, ffn)), full((1, ffn)),
                  full((ffn, ffn)), full((1, ffn)), full((1, ffn)),
                  pl.BlockSpec((ffn, c), lambda i, p: (0, p)), pl.BlockSpec((1, c), lambda i, p: (0, p))],
        out_specs=pl.BlockSpec((1, SUBLANES, FFT_N1 * c), lambda i, p: (p, i, 0)),
        out_shape=jax.ShapeDtypeStruct((nparts, seq // FFT_N1, FFT_N1 * c), F32),
        scratch_shapes=[pltpu.VMEM((c // LANES, tt, LANES), F32), pltpu.VMEM((tt, ffn), F32)],
        compiler_params=_cparams(2),
        name="hyena_filters",
    )(freq, w1p, b1.reshape(1, ffn), f1.reshape(1, ffn), w2, b2.reshape(1, ffn), f2.reshape(1, ffn),
      w3, decay.reshape(1, -1))


def _fft_tables(seq):
    n2h = seq // FFT_N1
    n2 = 2 * n2h
    n = FFT_N1 * n2
    k2n = n2h + 1
    k2p = _round_up(k2n, BF16_ROWS)
    kk = np.arange(k2p)[:, None]
    nn = np.arange(n2h)[None, :]
    ang = 2.0 * np.pi * ((kk * nn) % n2) / n2
    valid = (kk < k2n).astype(np.float64)
    fa = np.concatenate([np.cos(ang) * valid, -np.sin(ang) * valid], axis=0)
    wk = np.where((kk == 0) | (kk == n2h), 1.0, 2.0) * valid
    gc = np.concatenate([(np.cos(ang) * wk).T, (-np.sin(ang) * wk).T], axis=1)
    n1 = np.arange(FFT_N1)[None, :]
    tw_ang = 2.0 * np.pi * ((kk * n1) % n) / n
    twr = (np.cos(tw_ang) * valid).reshape(k2p, 1, FFT_N1)
    twi = (-np.sin(tw_ang) * valid).reshape(k2p, 1, FFT_N1)
    k1 = np.arange(FFT_N1)[:, None]
    base = 2.0 * np.pi * ((k1 * n1) % FFT_N1) / FFT_N1
    return dict(n2h=n2h, k2p=k2p, n=n,
                fa=jnp.asarray(fa, BF16), gc=jnp.asarray(gc, BF16),
                twr=jnp.asarray(twr, F32), twi=jnp.asarray(twi, F32),
                cr=jnp.asarray(np.cos(base), F32), ci=jnp.asarray(-np.sin(base), F32))


def _pack_pair(re, im):
    hi = lax.bitcast_convert_type(re.astype(BF16).astype(F32), U32)
    lo = lax.bitcast_convert_type(im.astype(BF16).astype(F32), U32)
    return hi | (lo >> 16)


def _unpack_pair(w):
    re = lax.bitcast_convert_type(w & jnp.uint32(0xFFFF0000), F32)
    im = lax.bitcast_convert_type(w << 16, F32)
    return re, im


def _ffta_kernel(x_ref, fa_ref, o_ref, *, k2p, c):
    for j in range(SUBLANES):
        xs = x_ref[0, :, j * c:(j + 1) * c].astype(BF16)
        res = jnp.dot(fa_ref[...], xs, preferred_element_type=F32)
        w = _pack_pair(res[:k2p], res[k2p:])
        for ct in range(c // LANES):
            o_ref[0, 0, ct, pl.ds(j, k2p, stride=SUBLANES), :] = w[:, ct * LANES:(ct + 1) * LANES]


def _fft_stage_a(xf, fa, c):
    bx, n2h, _ = xf.shape
    k2p = fa.shape[0] // 2
    ngrp = FFT_N1 // SUBLANES
    nct = c // LANES
    return pl.pallas_call(
        functools.partial(_ffta_kernel, k2p=k2p, c=c),
        grid=(bx, ngrp),
        in_specs=[pl.BlockSpec((1, n2h, SUBLANES * c), lambda b, g: (b, 0, g)),
                  pl.BlockSpec((2 * k2p, n2h), lambda b, g: (0, 0))],
        out_specs=pl.BlockSpec((1, 1, nct, k2p * SUBLANES, LANES), lambda b, g: (b, g, 0, 0, 0)),
        out_shape=jax.ShapeDtypeStruct((bx, ngrp, nct, k2p * SUBLANES, LANES), U32),
        compiler_params=_cparams(2),
        name="fft_stage_a",
    )(xf, fa)


def _load_stage_b(a_ref, b):
    ngrp, nct = a_ref.shape[1], a_ref.shape[2]
    return jnp.concatenate([jnp.concatenate([a_ref[b, g, ct] for ct in range(nct)], axis=1)
                            for g in range(ngrp)], axis=0)


def _stage_b_matrix(twr_ref, twi_ref, cr_ref, ci_ref):
    tr = twr_ref[0]
    ti = twi_ref[0]
    cr = cr_ref[...]
    ci = ci_ref[...]
    mr = cr * tr - ci * ti
    mi = ci * tr + cr * ti
    return mr, mi


def _fftb_filter_kernel(a_ref, twr_ref, twi_ref, cr_ref, ci_ref, o_ref, *, scale):
    mr, mi = _stage_b_matrix(twr_ref, twi_ref, cr_ref, ci_ref)
    fb = jnp.concatenate([jnp.concatenate([mr, -mi], axis=1),
                          jnp.concatenate([mi, mr], axis=1)], axis=0).astype(BF16)
    nparts = a_ref.shape[0]
    c = a_ref.shape[2] * LANES
    hs = []
    for p in range(nparts):
        ar, ai = _unpack_pair(_load_stage_b(a_ref, p))
        a = jnp.concatenate([ar, ai], axis=0).astype(BF16)
        hs.append(jnp.dot(fb, a, preferred_element_type=F32))
    for order in range(nparts // 2):
        fwd, bwd = hs[order], hs[nparts // 2 + order]
        o_ref[0, 0, :, order * c:(order + 1) * c] = (fwd[:FFT_N1] + bwd[:FFT_N1]) * scale
        o_ref[0, 1, :, order * c:(order + 1) * c] = (fwd[FFT_N1:] - bwd[FFT_N1:]) * scale


def _fft_stage_b_filter(ap, tabs, scale):
    k2p = tabs["k2p"]
    nparts, ngrp, nct = ap.shape[:3]
    c = nct * LANES
    return pl.pallas_call(
        functools.partial(_fftb_filter_kernel, scale=scale),
        grid=(k2p,),
        in_specs=[pl.BlockSpec((nparts, ngrp, nct, SUBLANES, LANES), lambda k: (0, 0, 0, k, 0)),
                  pl.BlockSpec((1, 1, FFT_N1), lambda k: (k, 0, 0)),
                  pl.BlockSpec((1, 1, FFT_N1), lambda k: (k, 0, 0)),
                  pl.BlockSpec((FFT_N1, FFT_N1), lambda k: (0, 0)),
                  pl.BlockSpec((FFT_N1, FFT_N1), lambda k: (0, 0))],
        out_specs=pl.BlockSpec((1, 2, FFT_N1, nparts // 2 * c), lambda k: (k, 0, 0, 0)),
        out_shape=jax.ShapeDtypeStruct((k2p, 2, FFT_N1, nparts // 2 * c), F32),
        compiler_params=_cparams(1),
        name="fft_stage_b_filter",
    )(ap, tabs["twr"], tabs["twi"], tabs["cr"], tabs["ci"])


def _fftb_conv_kernel(a_ref, g_ref, twr_ref, twi_ref, cr_ref, ci_ref, o_ref):
    mr, mi = _stage_b_matrix(twr_ref, twi_ref, cr_ref, ci_ref)
    fb = jnp.concatenate([jnp.concatenate([mr, -mi], axis=1),
                          jnp.concatenate([mi, mr], axis=1)], axis=0).astype(BF16)
    mrt, mit = mr.T, mi.T
    fbt = jnp.concatenate([jnp.concatenate([mrt, mit], axis=1),
                           jnp.concatenate([-mit, mrt], axis=1)], axis=0).astype(BF16)
    gr = g_ref[0, 0]
    gi = g_ref[0, 1]
    ngrp, nct = a_ref.shape[1], a_ref.shape[2]
    for b in range(a_ref.shape[0]):
        ar, ai = _unpack_pair(_load_stage_b(a_ref, b))
        a = jnp.concatenate([ar, ai], axis=0).astype(BF16)
        xh = jnp.dot(fb, a, preferred_element_type=F32)
        xr, xi = xh[:FFT_N1], xh[FFT_N1:]
        y = jnp.concatenate([xr * gr - xi * gi, xr * gi + xi * gr], axis=0).astype(BF16)
        cc = jnp.dot(fbt, y, preferred_element_type=F32)
        w = _pack_pair(cc[:FFT_N1], cc[FFT_N1:])
        for g in range(ngrp):
            for ct in range(nct):
                o_ref[b, g, ct] = w[g * SUBLANES:(g + 1) * SUBLANES, ct * LANES:(ct + 1) * LANES]


def _fft_stage_b_conv(ap, g, order, tabs):
    bsz, ngrp, nct = ap.shape[:3]
    k2p = tabs["k2p"]
    c = nct * LANES
    blk = pl.BlockSpec((bsz, ngrp, nct, SUBLANES, LANES), lambda k: (0, 0, 0, k, 0))
    return pl.pallas_call(
        _fftb_conv_kernel,
        grid=(k2p,),
        in_specs=[blk,
                  pl.BlockSpec((1, 2, FFT_N1, c), lambda k: (k, 0, 0, order)),
                  pl.BlockSpec((1, 1, FFT_N1), lambda k: (k, 0, 0)),
                  pl.BlockSpec((1, 1, FFT_N1), lambda k: (k, 0, 0)),
                  pl.BlockSpec((FFT_N1, FFT_N1), lambda k: (0, 0)),
                  pl.BlockSpec((FFT_N1, FFT_N1), lambda k: (0, 0))],
        out_specs=blk,
        out_shape=jax.ShapeDtypeStruct(ap.shape, U32),
        compiler_params=_cparams(1),
        name="fft_stage_b_conv",
    )(ap, g, tabs["twr"], tabs["twi"], tabs["cr"], tabs["ci"])


def _fftc_kernel(c_ref, gc_ref, u_ref, gate_ref, d_ref, o_ref, *, k2p, c):
    for j in range(SUBLANES):
        w = jnp.concatenate([c_ref[0, 0, ct, pl.ds(j, k2p, stride=SUBLANES), :] for ct in range(c // LANES)],
                            axis=1)
        re, im = _unpack_pair(w)
        cm = jnp.concatenate([re, im], axis=0).astype(BF16)
        y = jnp.dot(gc_ref[...], cm, preferred_element_type=F32)
        sl = slice(j * c, (j + 1) * c)
        o_ref[0, :, sl] = gate_ref[0, :, sl] * (y + d_ref[...] * u_ref[0, :, sl])


def _fft_stage_c(cp, gc, uf, gatef, d_row):
    bsz, ngrp, nct = cp.shape[:3]
    n2h = uf.shape[1]
    c = nct * LANES
    k2p = gc.shape[1] // 2
    fblk = pl.BlockSpec((1, n2h, SUBLANES * c), lambda b, g: (b, 0, g))
    return pl.pallas_call(
        functools.partial(_fftc_kernel, k2p=k2p, c=c),
        grid=(bsz, ngrp),
        in_specs=[pl.BlockSpec((1, 1, nct, k2p * SUBLANES, LANES), lambda b, g: (b, g, 0, 0, 0)),
                  pl.BlockSpec((n2h, 2 * k2p), lambda b, g: (0, 0)),
                  fblk, fblk,
                  pl.BlockSpec((1, c), lambda b, g: (0, 0))],
        out_specs=fblk,
        out_shape=jax.ShapeDtypeStruct(uf.shape, F32),
        compiler_params=_cparams(2),
        name="fft_stage_c",
    )(cp, gc, uf, gatef, d_row)


def _hyena_mix(hv, hx1, hx2, filt, skip, seq):
    c = skip.shape[1]
    tabs = _fft_tables(seq)
    g = _fft_stage_b_filter(_fft_stage_a(filt, tabs["fa"], c), tabs, 1.0 / tabs["n"])

    def conv(u, gate, order):
        cp = _fft_stage_b_conv(_fft_stage_a(u, tabs["fa"], c), g, order, tabs)
        return _fft_stage_c(cp, tabs["gc"], u, gate, skip[order].reshape(1, c))

    return conv(conv(hv, hx1, 0), hx2, 1)


def _qkprep_kernel(q_ref, k_ref, v_ref, qn_ref, kn_ref, oqt_ref, ok_ref, ovt_ref, *, tt, seq):
    i = pl.program_id(0)
    t = (i * tt + lax.broadcasted_iota(I32, (tt, HEAD_DIM), 0)) % seq
    lane = lax.broadcasted_iota(I32, (tt, HEAD_DIM), 1)
    j = lane % (HEAD_DIM // 2)
    quarter = HEAD_DIM // 4
    f = (j % quarter).astype(F32)
    inv = jnp.exp(f * (-2.0 * math.log(ROPE_THETA) / (HEAD_DIM // 2)))
    pos = jnp.where(j < quarter, t // GRID_W, t % GRID_W).astype(F32)
    ang = pos * inv
    cosv = jnp.cos(ang)
    sinv = jnp.sin(ang)
    sin_signed = jnp.where(lane < HEAD_DIM // 2, -sinv, sinv)

    def norm_rope(x, g, scale):
        ms = jnp.mean(x * x, axis=-1, keepdims=True)
        xn = x * lax.rsqrt(ms + NORM_EPS) * g
        return (xn * cosv + pltpu.roll(xn, HEAD_DIM // 2, axis=1) * sin_signed) * scale

    qg = qn_ref[...]
    kg = kn_ref[...]
    q_scale = LOG2_E * HEAD_DIM ** -0.5
    for h in range(N_HEADS):
        sl = slice(h * HEAD_DIM, (h + 1) * HEAD_DIM)
        oqt_ref[h] = norm_rope(q_ref[:, sl], qg, q_scale).T.astype(BF16)
    for h in range(N_KV_HEADS):
        sl = slice(h * HEAD_DIM, (h + 1) * HEAD_DIM)
        ok_ref[:, sl] = norm_rope(k_ref[:, sl], kg, 1.0).astype(BF16)
        ovt_ref[h, :HEAD_DIM, :] = v_ref[:, sl].T.astype(BF16)
        extra = lax.broadcasted_iota(I32, (V_ROWS - HEAD_DIM, tt), 0)
        ovt_ref[h, HEAD_DIM:, :] = jnp.where(extra == 0, 1.0, 0.0).astype(BF16)


def _qk_prep(proj, qn, kn, seq):
    n = proj.shape[0]
    tt = min(512, seq)
    qw = N_HEADS * HEAD_DIM
    kw = N_KV_HEADS * HEAD_DIM
    q_off = (3 * HY_WIDTH) // qw
    k_off = (3 * HY_WIDTH + qw) // kw
    return pl.pallas_call(
        functools.partial(_qkprep_kernel, tt=tt, seq=seq),
        grid=(n // tt,),
        in_specs=[pl.BlockSpec((tt, qw), lambda i: (i, q_off)),
                  pl.BlockSpec((tt, kw), lambda i: (i, k_off)),
                  pl.BlockSpec((tt, kw), lambda i: (i, k_off + 1)),
                  pl.BlockSpec((1, HEAD_DIM), lambda i: (0, 0)),
                  pl.BlockSpec((1, HEAD_DIM), lambda i: (0, 0))],
        out_specs=[pl.BlockSpec((N_HEADS, HEAD_DIM, tt), lambda i: (0, 0, i)),
                   pl.BlockSpec((tt, kw), lambda i: (i, 0)),
                   pl.BlockSpec((N_KV_HEADS, V_ROWS, tt), lambda i: (0, 0, i))],
        out_shape=[jax.ShapeDtypeStruct((N_HEADS, HEAD_DIM, n), BF16),
                   jax.ShapeDtypeStruct((n, kw), BF16),
                   jax.ShapeDtypeStruct((N_KV_HEADS, V_ROWS, n), BF16)],
        compiler_params=_cparams(1),
        name="qk_prep",
    )(proj, proj, proj, qn, kn)


def _flash_kernel(qt_ref, k_ref, vt_ref, o_ref, acc_sc, s_sc, mc_sc, *, tq, tk, seq, unroll):
    nq = Q_PER_KV * tq
    nk = seq // tk
    qt = jnp.concatenate([qt_ref[g] for g in range(Q_PER_KV)], axis=1)
    acc_sc[...] = jnp.zeros(acc_sc.shape, F32)

    def scores(c, slot):
        start = pl.multiple_of(c * tk, tk)
        s = jnp.dot(k_ref[pl.ds(start, tk), :], qt, preferred_element_type=F32)
        s_sc[slot] = s
        mc_sc[slot] = jnp.max(s, axis=0, keepdims=True)

    scores(0, 0)

    def body(i, m):
        for u in range(unroll):
            c = i * unroll + u
            scores(jnp.minimum(c + 1, nk - 1), (u + 1) % 2)
            s = s_sc[u % 2]
            mc = mc_sc[u % 2]
            start = pl.multiple_of(c * tk, tk)
            vc = vt_ref[0, :, pl.ds(start, tk)]
            p = jnp.exp2(s - mc).astype(BF16)
            oc = jnp.dot(vc, p, preferred_element_type=F32)
            m_new = jnp.maximum(m, mc)
            acc_sc[...] = acc_sc[...] * jnp.exp2(m - m_new) + oc * jnp.exp2(mc - m_new)
            m = m_new
        return m

    lax.fori_loop(0, nk // unroll, body, jnp.full((1, nq), -jnp.inf, F32))
    o = acc_sc[:HEAD_DIM, :] / acc_sc[HEAD_DIM:HEAD_DIM + 1, :]
    for g in range(Q_PER_KV):
        o_ref[:, g * HEAD_DIM:(g + 1) * HEAD_DIM] = o[:, g * tq:(g + 1) * tq].T


def _flash_attention(qt, k, vt, bsz, seq):
    n = bsz * seq
    tq = min(256, seq)
    tk = min(1024, seq // 2)
    unroll = 2 if seq // tk >= 2 else 1
    gw = Q_PER_KV * HEAD_DIM
    nqb = seq // tq
    return pl.pallas_call(
        functools.partial(_flash_kernel, tq=tq, tk=tk, seq=seq, unroll=unroll),
        grid=(bsz, N_KV_HEADS, nqb),
        in_specs=[pl.BlockSpec((Q_PER_KV, HEAD_DIM, tq), lambda b, h, i: (h, 0, b * nqb + i)),
                  pl.BlockSpec((seq, HEAD_DIM), lambda b, h, i: (b, h)),
                  pl.BlockSpec((1, V_ROWS, seq), lambda b, h, i: (h, 0, b))],
        out_specs=pl.BlockSpec((tq, gw), lambda b, h, i: (b * nqb + i, h)),
        out_shape=jax.ShapeDtypeStruct((n, N_HEADS * HEAD_DIM), F32),
        scratch_shapes=[pltpu.VMEM((V_ROWS, Q_PER_KV * tq), F32),
                        pltpu.VMEM((2, tk, Q_PER_KV * tq), F32),
                        pltpu.VMEM((2, 1, Q_PER_KV * tq), F32)],
        compiler_params=_cparams(3),
        name="flash_attention",
    )(qt, k, vt)


def _layer_norm(y, g, b):
    mu = jnp.mean(y, axis=-1, keepdims=True)
    yc = y - mu
    var = jnp.mean(yc * yc, axis=-1, keepdims=True)
    return yc * lax.rsqrt(var + NORM_EPS) * g + b


def _merge_kernel(yh_ref, ya_ref, x_ref, gh_ref, ga_ref, wo_ref, lg_ref, lb_ref, wr_ref, br_ref,
                  x1_ref, aff_ref):
    hw = yh_ref.shape[1]
    yh = yh_ref[...]
    ya = ya_ref[...]
    mh = yh * lax.rsqrt(jnp.mean(yh * yh, axis=-1, keepdims=True) + NORM_EPS) * gh_ref[...]
    ma = ya * lax.rsqrt(jnp.mean(ya * ya, axis=-1, keepdims=True) + NORM_EPS) * ga_ref[...]
    y = (jnp.dot(mh.astype(BF16), wo_ref[:hw, :], preferred_element_type=F32)
         + jnp.dot(ma.astype(BF16), wo_ref[hw:, :], preferred_element_type=F32)
         + DN_ALPHA * x_ref[...])
    x1 = _layer_norm(y, lg_ref[...], lb_ref[...])
    x1_ref[...] = x1
    x_hi = x1.astype(BF16)
    x_lo = (x1 - x_hi.astype(F32)).astype(BF16)
    both = jnp.dot(x_hi, wr_ref[...], preferred_element_type=F32)
    logits = (both[:, :LANES] + both[:, LANES:]
              + jnp.dot(x_lo, wr_ref[:, :LANES], preferred_element_type=F32) + br_ref[...])
    e = jnp.exp(logits - jnp.max(logits, axis=-1, keepdims=True))
    aff_ref[...] = e / jnp.sum(e, axis=-1, keepdims=True)


def _merge(y_hy, y_att, x2, g_hy, g_attn, wo_bf, ln_g, ln_b, wr_pad, br_pad):
    n, d = x2.shape
    hw = y_hy.shape[1]
    tm = min(256, n)
    row = lambda w: pl.BlockSpec((1, w), lambda i: (0, 0))
    return pl.pallas_call(
        _merge_kernel,
        grid=(n // tm,),
        in_specs=[pl.BlockSpec((tm, hw), lambda i: (i, 0)),
                  pl.BlockSpec((tm, d - hw), lambda i: (i, 0)),
                  pl.BlockSpec((tm, d), lambda i: (i, 0)),
                  row(hw), row(d - hw),
                  pl.BlockSpec((d, d), lambda i: (0, 0)),
                  row(d), row(d),
                  pl.BlockSpec((d, 2 * LANES), lambda i: (0, 0)),
                  row(LANES)],
        out_specs=[pl.BlockSpec((tm, d), lambda i: (i, 0)),
                   pl.BlockSpec((tm, LANES), lambda i: (i, 0))],
        out_shape=[jax.ShapeDtypeStruct((n, d), F32), jax.ShapeDtypeStruct((n, LANES), F32)],
        compiler_params=_cparams(1),
        name="merge_ln_router",
    )(y_hy, y_att, x2, g_hy, g_attn, wo_bf, ln_g, ln_b, wr_pad, br_pad)


def _select_kernel(a_ref, idx_ref, gslot_ref, pos_ref, excl_ref, *, cap):
    a = a_ref[0]
    ne, nr, _ = a.shape
    bits = lax.bitcast_convert_type(a, I32)

    def count(mask):
        return jnp.sum(jnp.sum(mask.astype(F32), axis=2, keepdims=True), axis=1, keepdims=True)

    thr = jnp.zeros((ne, 1, 1), I32)
    for bit in range(30, -1, -1):
        cand = thr | (1 << bit)
        thr = jnp.where(count(bits >= cand) >= cap, cand, thr)

    ii = lax.broadcasted_iota(I32, (LANES, LANES), 0)
    jj = lax.broadcasted_iota(I32, (LANES, LANES), 1)
    upper = (ii <= jj).astype(BF16)
    ri = lax.broadcasted_iota(I32, (ne, nr, nr), 1)
    rj = lax.broadcasted_iota(I32, (ne, nr, nr), 2)
    lower_strict = (rj < ri).astype(BF16)

    def prefix(mask):
        mb = mask.astype(BF16).reshape(ne * nr, LANES)
        pin = jnp.dot(mb, upper, preferred_element_type=F32).reshape(ne, nr, LANES)
        tot = pin[:, :, LANES - 1:LANES]
        totb = jnp.broadcast_to(tot, (ne, nr, LANES)).astype(BF16)
        off = jnp.einsum("ers,esl->erl", lower_strict, totb, preferred_element_type=F32)
        return pin, tot, off[:, :, 0:1]

    gt = bits > thr
    eq = bits == thr
    need = cap - count(gt)
    pin_eq, _, off_eq = prefix(eq)
    eq_rank = pin_eq + off_eq - eq.astype(F32)
    chosen = gt | (eq & (eq_rank < need))
    pin, tot, off = prefix(chosen)
    chosen_f = chosen.astype(F32)
    excl = pin + off - chosen_f
    excl_ref[0] = excl
    pos_ref[0] = jnp.where(chosen, excl, -1.0)

    s_row = lax.broadcasted_iota(I32, (nr, cap), 1).astype(F32)
    r_col = lax.broadcasted_iota(I32, (nr, cap), 0).astype(F32)
    j_col = lax.broadcasted_iota(I32, (LANES, cap), 0).astype(F32)
    for e in range(ne):
        lo = off[e]
        hi = lo + tot[e]
        in_row = ((lo <= s_row) & (s_row < hi)).astype(F32)
        in_row_b = in_row.astype(BF16)
        row_of_s = jnp.sum(in_row * r_col, axis=0, keepdims=True)
        s_local = s_row[0:1] - jnp.sum(in_row * lo, axis=0, keepdims=True)
        pin_t = pin[e].T.astype(BF16)
        g = jnp.dot(pin_t, in_row_b, preferred_element_type=F32)
        lane_of_s = jnp.sum((g <= s_local).astype(F32), axis=0, keepdims=True)
        idx_ref[0, e:e + 1, :] = (row_of_s * LANES + lane_of_s).astype(I32)
        rem = a[e]
        a_of_s = jnp.zeros((LANES, cap), F32)
        for _ in range(3):
            part = rem.astype(BF16)
            rem = rem - part.astype(F32)
            a_of_s = a_of_s + jnp.dot(part.astype(F32).T.astype(BF16), in_row_b, preferred_element_type=F32)
        gslot_ref[0, e:e + 1, :] = jnp.sum(jnp.where(j_col == lane_of_s, a_of_s, 0.0), axis=0, keepdims=True)


def _select(aff4, cap):
    bsz, ne, nr, _ = aff4.shape
    blk = pl.BlockSpec((1, ne, nr, LANES), lambda b: (b, 0, 0, 0))
    slot = pl.BlockSpec((1, ne, cap), lambda b: (b, 0, 0))
    return pl.pallas_call(
        functools.partial(_select_kernel, cap=cap),
        grid=(bsz,),
        in_specs=[blk],
        out_specs=[slot, slot, blk, blk],
        out_shape=[jax.ShapeDtypeStruct((bsz, ne, cap), I32),
                   jax.ShapeDtypeStruct((bsz, ne, cap), F32),
                   jax.ShapeDtypeStruct(aff4.shape, F32),
                   jax.ShapeDtypeStruct(aff4.shape, F32)],
        compiler_params=_cparams(1),
        name="expert_choice_select",
    )(aff4)


def _expert_kernel(idx_ref, idxn_ref, x_hbm, g_ref, wg_ref, wu_ref, wd_ref, o_ref, xg_ref, sem,
                   *, tc, seq, ff, nj, bsz, nsteps):
    s = pl.program_id(0)
    slot = s % 2

    def issue(ids_ref, step, to_slot):
        base = ((step // nj) % bsz) * seq
        for r in range(tc):
            pltpu.make_async_copy(x_hbm.at[pl.ds(base + ids_ref[0, 0, r], 1)],
                                  xg_ref.at[to_slot, pl.ds(r, 1)], sem.at[to_slot]).start()

    def wait_rows(of_slot):
        pltpu.make_async_copy(x_hbm.at[pl.ds(0, tc)], xg_ref.at[of_slot], sem.at[of_slot]).wait()

    @pl.when(s == 0)
    def _():
        issue(idx_ref, s, 0)

    issue(idxn_ref, jnp.minimum(s + 1, nsteps - 1), 1 - slot)
    wait_rows(slot)
    xb = xg_ref[slot].astype(BF16)
    gate = jnp.dot(xb, wg_ref[0], preferred_element_type=F32)
    h = gate * jax.nn.sigmoid(gate) * jnp.dot(xb, wu_ref[0], preferred_element_type=F32)
    y = jnp.dot(h.astype(BF16), wd_ref[0], preferred_element_type=F32)
    o_ref[0, 0] = (y * g_ref[0]).astype(BF16)

    @pl.when(s == nsteps - 1)
    def _():
        wait_rows(1 - slot)


def _experts(idx, gslot, x1, wg_bf, wu_bf, wd_bf, bsz, seq):
    _, ne, cap = idx.shape
    d = x1.shape[1]
    ff = wd_bf.shape[1]
    tc = min(256, cap)
    nj = cap // tc
    nsteps = ne * bsz * nj
    idx_s = idx.transpose(1, 0, 2).reshape(nsteps, 1, tc)
    g_s = gslot.transpose(1, 0, 2).reshape(nsteps, tc, 1)
    out_map = lambda s: ((s // nj) % bsz, s // (bsz * nj), s % nj, 0)
    return pl.pallas_call(
        functools.partial(_expert_kernel, tc=tc, seq=seq, ff=ff, nj=nj, bsz=bsz, nsteps=nsteps),
        grid=(nsteps,),
        in_specs=[pl.BlockSpec((1, 1, tc), lambda s: (s, 0, 0), memory_space=pltpu.SMEM),
                  pl.BlockSpec((1, 1, tc), lambda s: (jnp.minimum(s + 1, nsteps - 1), 0, 0),
                               memory_space=pltpu.SMEM),
                  pl.BlockSpec(memory_space=pl.ANY),
                  pl.BlockSpec((1, tc, 1), lambda s: (s, 0, 0)),
                  pl.BlockSpec((1, d, ff), lambda s: (s // (bsz * nj), 0, 0)),
                  pl.BlockSpec((1, d, ff), lambda s: (s // (bsz * nj), 0, 0)),
                  pl.BlockSpec((1, ff, d), lambda s: (s // (bsz * nj), 0, 0))],
        out_specs=pl.BlockSpec((1, 1, tc, d), out_map),
        out_shape=jax.ShapeDtypeStruct((bsz, ne, cap, d), BF16),
        scratch_shapes=[pltpu.VMEM((2, tc, d), F32), pltpu.SemaphoreType.DMA((2,))],
        compiler_params=_cparams(1),
        name="expert_ffn",
    )(idx_s, idx_s, x1, g_s, wg_bf, wu_bf, wd_bf)


def _combine_kernel(w0_ref, y_hbm, x1_ref, pos_ref, lg_ref, lb_ref, o_ref, ybuf, xbuf, acc_ref, sem, xsem,
                    *, tb_tokens, win, cap, ne, ntb):
    s = pl.program_id(0)
    nsteps = pl.num_programs(0)
    slot = s % 2

    def window_copies(step, to_slot):
        bb = step // ntb
        wbase = (bb * (ntb + 1) + step % ntb) * ne
        copies, firsts = [], []
        for e in range(ne):
            sa = jnp.minimum((w0_ref[wbase + e] // BF16_ROWS) * BF16_ROWS, cap - win)
            sa = pl.multiple_of(sa, BF16_ROWS)
            firsts.append(sa)
            copies.append(pltpu.make_async_copy(y_hbm.at[bb, e, pl.ds(sa, win)],
                                                ybuf.at[to_slot, pl.ds(e * win, win)], sem.at[to_slot, e]))
        return copies, firsts

    @pl.when(s == 0)
    def _():
        for cp in window_copies(s, 0)[0]:
            cp.start()

    @pl.when(s + 1 < nsteps)
    def _():
        for cp in window_copies(s + 1, 1 - slot)[0]:
            cp.start()

    b = s // ntb
    base = (b * (ntb + 1) + s % ntb) * ne
    copies, starts = window_copies(s, slot)
    lane = lax.broadcasted_iota(I32, (tb_tokens, win), 1).astype(F32)
    onehots = [(pos_ref[0, :, e:e + 1] - starts[e].astype(F32) == lane).astype(BF16) for e in range(ne)]
    p = jnp.concatenate(onehots, axis=1)
    for cp in copies:
        cp.wait()
    acc_ref[...] = DN_ALPHA * x1_ref[...] + jnp.dot(p, ybuf[slot], preferred_element_type=F32)

    for e in range(ne):
        first_end = starts[e] + win
        n_extra = jnp.maximum(w0_ref[base + ne + e] - first_end + win - 1, 0) // win

        def extra(w, carry, e=e, first_end=first_end):
            nominal = first_end + w * win
            st = pl.multiple_of(jnp.minimum(nominal, cap - win), BF16_ROWS)
            cp = pltpu.make_async_copy(y_hbm.at[b, e, pl.ds(st, win)], xbuf, xsem)
            cp.start()
            cp.wait()
            pe = pos_ref[0, :, e:e + 1]
            oh = ((pe - st.astype(F32) == lane) & (pe >= nominal.astype(F32))).astype(BF16)
            acc_ref[...] += jnp.dot(oh, xbuf[...], preferred_element_type=F32)
            return carry

        lax.fori_loop(0, n_extra, extra, 0)
    o_ref[...] = _layer_norm(acc_ref[...], lg_ref[...], lb_ref[...])


def _combine(y, x1, pos_t, w0, ln_g, ln_b, bsz, seq, tbt):
    _, ne, cap, d = y.shape
    win = min(LANES, cap)
    ntb = seq // tbt
    grid_spec = pltpu.PrefetchScalarGridSpec(
        num_scalar_prefetch=1,
        grid=(bsz * ntb,),
        in_specs=[pl.BlockSpec(memory_space=pl.ANY),
                  pl.BlockSpec((tbt, d), lambda s, w: (s, 0)),
                  pl.BlockSpec((1, tbt, ne), lambda s, w: (s // ntb, s % ntb, 0)),
                  pl.BlockSpec((1, d), lambda s, w: (0, 0)),
                  pl.BlockSpec((1, d), lambda s, w: (0, 0))],
        out_specs=pl.BlockSpec((tbt, d), lambda s, w: (s, 0)),
        scratch_shapes=[pltpu.VMEM((2, ne * win, d), BF16), pltpu.VMEM((win, d), BF16),
                        pltpu.VMEM((tbt, d), F32),
                        pltpu.SemaphoreType.DMA((2, ne)), pltpu.SemaphoreType.DMA(())],
    )
    return pl.pallas_call(
        functools.partial(_combine_kernel, tb_tokens=tbt, win=win, cap=cap, ne=ne, ntb=ntb),
        grid_spec=grid_spec,
        out_shape=jax.ShapeDtypeStruct((bsz * seq, d), F32),
        compiler_params=_cparams(1),
        name="moe_combine_ln",
    )(w0, y, x1, pos_t, ln_g, ln_b)


def _deinterleave(width):
    idx = np.arange(width).reshape(-1, HEAD_DIM // 2, 2)
    return np.concatenate([idx[:, :, 0], idx[:, :, 1]], axis=1).reshape(-1)


def kernel(x, w_in, b_in, hy_conv_w, hy_conv_b, hy_ffn_w1, hy_ffn_b1, hy_sin_f1, hy_ffn_w2, hy_ffn_b2, hy_sin_f2, hy_ffn_w3, hy_decay, hy_skip, q_norm, k_norm, g_hy, g_attn, w_out, ln1_g, ln1_b, w_router, b_router, w_gate, w_up, w_down, ln2_g, ln2_b):
    bsz, seq, d = x.shape
    n = bsz * seq
    x2 = x.reshape(n, d)
    s1 = 3 * HY_WIDTH
    qw = N_HEADS * HEAD_DIM
    kw = N_KV_HEADS * HEAD_DIM
    perm = np.concatenate([np.arange(s1), s1 + _deinterleave(qw), s1 + qw + _deinterleave(kw),
                           np.arange(s1 + qw + kw, w_in.shape[1])])
    w_in_bf = w_in[:, perm].astype(BF16)
    b_in_p = b_in[perm].reshape(1, -1)
    hd_perm = _deinterleave(HEAD_DIM)

    proj = _in_projection(x2, w_in_bf, b_in_p)

    conv_b = hy_conv_b.reshape(1, -1)
    hv, hx1, hx2 = [_short_conv(proj, hy_conv_w, conv_b, part, bsz, seq) for part in range(3)]
    filt = _hyena_filters(seq, hy_ffn_w1, hy_ffn_b1, hy_sin_f1, hy_ffn_w2, hy_ffn_b2, hy_sin_f2,
                          hy_ffn_w3, hy_decay)
    y_hy = _hyena_mix(hv, hx1, hx2, filt, hy_skip, seq).reshape(n, HY_WIDTH)

    qt, kb, vt = _qk_prep(proj, q_norm[hd_perm].reshape(1, HEAD_DIM), k_norm[hd_perm].reshape(1, HEAD_DIM), seq)
    y_att = _flash_attention(qt, kb, vt, bsz, seq)

    ne = w_router.shape[1]
    wr_hi = w_router.astype(BF16)
    wr_lo = (w_router - wr_hi.astype(F32)).astype(BF16)
    wr_pair = (jnp.zeros((d, 2 * LANES), BF16).at[:, :ne].set(wr_hi).at[:, LANES:LANES + ne].set(wr_lo))
    br_pad = jnp.full((1, LANES), -1e30, F32).at[0, :ne].set(b_router)
    x1, aff = _merge(y_hy, y_att, x2, g_hy.reshape(1, -1), g_attn.reshape(1, -1),
                     w_out.astype(BF16), ln1_g.reshape(1, d), ln1_b.reshape(1, d), wr_pair, br_pad)

    cap = CAPACITY_FACTOR * seq // ne
    aff4 = aff[:, :ne].reshape(bsz, seq, ne).transpose(0, 2, 1).reshape(bsz, ne, seq // LANES, LANES)
    idx, gslot, pos, excl = _select(aff4, cap)
    y = _experts(idx, gslot, x1, w_gate.astype(BF16), w_up.astype(BF16), w_down.astype(BF16), bsz, seq)
    pos_t = pos.reshape(bsz, ne, seq).transpose(0, 2, 1)
    tbt = min(256, cap)
    before = excl.reshape(bsz, ne, seq // tbt, tbt)[:, :, :, 0].transpose(0, 2, 1)
    w0 = jnp.concatenate([before, jnp.full((bsz, 1, ne), cap, F32)], axis=1).reshape(-1).astype(I32)
    out = _combine(y, x1, pos_t, w0, ln2_g.reshape(1, d), ln2_b.reshape(1, d), bsz, seq, tbt)
    return out.reshape(bsz, seq, d)
```

```python
import functools
import math

import numpy as np
import jax
import jax.numpy as jnp
from jax import lax
from jax.experimental import pallas as pl
from jax.experimental.pallas import tpu as pltpu

F32 = jnp.float32
BF16 = jnp.bfloat16
I32 = jnp.int32
U32 = jnp.uint32
HIGHEST = lax.Precision.HIGHEST

HY_WIDTH = 1024
HY_BANDS = 16
HEAD_DIM = 128
N_HEADS = 8
N_KV_HEADS = 2
Q_PER_KV = N_HEADS // N_KV_HEADS
ROPE_THETA = 10000.0
GRID_W = 64
N_EXPERTS = 16
CAPACITY_FACTOR = 2
NORM_EPS = 1e-6
DN_ALPHA = 2.0 ** 0.25
LOG2_E = math.log2(math.e)
V_ROWS = HEAD_DIM + 16

LANES = 128
SUBLANES = 8
BF16_ROWS = 16
FFT_N1 = 128
VMEM_LIMIT = 56 * 1024 * 1024


def _cparams(n_axes, vmem=VMEM_LIMIT):
    return pltpu.CompilerParams(dimension_semantics=("arbitrary",) * n_axes, vmem_limit_bytes=vmem)


def _round_up(a, m):
    return (a + m - 1) // m * m


def _inproj_kernel(x_ref, w_ref, b_ref, o_ref, xb_ref):
    @pl.when(pl.program_id(1) == 0)
    def _():
        xb_ref[...] = x_ref[...].astype(BF16)

    o_ref[...] = jnp.dot(xb_ref[...], w_ref[...], preferred_element_type=F32) + b_ref[...]


def _in_projection(x2, w_bf, b_row):
    n, d = x2.shape
    wdt = w_bf.shape[1]
    tm = min(1024, n)
    tn = wdt // 3
    return pl.pallas_call(
        _inproj_kernel,
        grid=(n // tm, wdt // tn),
        in_specs=[pl.BlockSpec((tm, d), lambda i, j: (i, 0)),
                  pl.BlockSpec((d, tn), lambda i, j: (0, j)),
                  pl.BlockSpec((1, tn), lambda i, j: (0, j))],
        out_specs=pl.BlockSpec((tm, tn), lambda i, j: (i, j)),
        out_shape=jax.ShapeDtypeStruct((n, wdt), F32),
        scratch_shapes=[pltpu.VMEM((tm, d), BF16)],
        compiler_params=_cparams(2),
        name="in_projection",
    )(x2, w_bf, b_row)


def _store_fft_layout(res, o_ref, scr):
    tt, c = res.shape
    nct = c // LANES
    for ct in range(nct):
        scr[ct] = res[:, ct * LANES:(ct + 1) * LANES]

    for n1 in range(FFT_N1):
        for ct in range(nct):
            col = n1 * c + ct * LANES
            o_ref[0, :, col:col + LANES] = scr[ct, pl.ds(n1, tt // FFT_N1, stride=FFT_N1), :]


def _shortconv_kernel(m_ref, p_ref, nx_ref, w_ref, b_ref, o_ref, scr, *, tt, seq):
    i = pl.program_id(0)
    r_in = lax.broadcasted_iota(I32, (tt, 1), 0)
    tpos = (i * tt + r_in) % seq
    cur = m_ref[...]
    up = pltpu.roll(cur, 1, axis=0)
    up = jnp.where(r_in == 0, p_ref[SUBLANES - 1:SUBLANES, :], up)
    up = jnp.where(tpos == 0, 0.0, up)
    dn = pltpu.roll(cur, tt - 1, axis=0)
    dn = jnp.where(r_in == tt - 1, nx_ref[0:1, :], dn)
    dn = jnp.where(tpos == seq - 1, 0.0, dn)
    wv = w_ref[...]
    _store_fft_layout(wv[0:1] * up + wv[1:2] * cur + wv[2:3] * dn + b_ref[...], o_ref, scr)


def _short_conv(proj, conv_w, conv_b, part, bsz, seq):
    n = proj.shape[0]
    c = HY_WIDTH
    tt = SUBLANES * FFT_N1
    nrb = n // SUBLANES
    hb = tt // SUBLANES
    tps = seq // tt
    return pl.pallas_call(
        functools.partial(_shortconv_kernel, tt=tt, seq=seq),
        grid=(n // tt,),
        in_specs=[pl.BlockSpec((tt, c), lambda i: (i, part)),
                  pl.BlockSpec((SUBLANES, c), lambda i: (jnp.maximum(i * hb - 1, 0), part)),
                  pl.BlockSpec((SUBLANES, c), lambda i: (jnp.minimum((i + 1) * hb, nrb - 1), part)),
                  pl.BlockSpec((3, c), lambda i: (0, part)),
                  pl.BlockSpec((1, c), lambda i: (0, part))],
        out_specs=pl.BlockSpec((1, SUBLANES, FFT_N1 * c), lambda i: (i // tps, i % tps, 0)),
        out_shape=jax.ShapeDtypeStruct((bsz, seq // FFT_N1, FFT_N1 * c), F32),
        scratch_shapes=[pltpu.VMEM((c // LANES, tt, LANES), F32)],
        compiler_params=_cparams(1),
        name="short_conv",
    )(proj, proj, proj, conv_w, conv_b)


def _filter_kernel(freq_ref, w1_ref, b1_ref, f1_ref, w2_ref, b2_ref, f2_ref, w3_ref, dec_ref, o_ref, scr, h_sc,
                   *, tt, seq, nparts):
    i = pl.program_id(0)
    part = pl.program_id(1)
    @pl.when(part == 0)
    def _():
        r = (i * tt + lax.broadcasted_iota(I32, (tt, LANES), 0)).astype(F32)
        lane = lax.broadcasted_iota(I32, (tt, LANES), 1)
        t = r / float(seq - 1)
        w = (2.0 * math.pi / seq) * r
        arg = freq_ref[...] * w
        emb = jnp.where(lane == 0, t,
                        jnp.where(lane <= HY_BANDS, jnp.cos(arg),
                                  jnp.where(lane <= 2 * HY_BANDS, -jnp.sin(arg), 0.0)))
        h = jnp.sin(f1_ref[...] * (jnp.dot(emb, w1_ref[...], preferred_element_type=F32, precision=HIGHEST)
                                   + b1_ref[...]))
        h_sc[...] = jnp.sin(f2_ref[...] * (jnp.dot(h, w2_ref[...], preferred_element_type=F32,
                                                   precision=HIGHEST) + b2_ref[...]))

    hf = h_sc[...]
    wf = w3_ref[...]
    h_hi = hf.astype(BF16)
    h_lo = (hf - h_hi.astype(F32)).astype(BF16)
    w_hi = wf.astype(BF16)
    w_lo = (wf - w_hi.astype(F32)).astype(BF16)
    h3 = (jnp.dot(h_hi, w_hi, preferred_element_type=F32) + jnp.dot(h_lo, w_hi, preferred_element_type=F32)
          + jnp.dot(h_hi, w_lo, preferred_element_type=F32))
    t_col = (i * tt + lax.broadcasted_iota(I32, (tt, 1), 0)).astype(F32) / float(seq - 1)
    win = jnp.exp(-t_col * jnp.abs(dec_ref[...]))
    out = h3 * win
    row = i * tt + lax.broadcasted_iota(I32, out.shape, 0)
    out = jnp.where((row == 0) & (part >= nparts // 2), 0.0, out)
    _store_fft_layout(out, o_ref, scr)


def _hyena_filters(seq, w1, b1, f1, w2, b2, f2, w3, decay):
    ffn = w1.shape[1]
    c = HY_WIDTH
    nparts = w3.shape[1] // c
    tt = SUBLANES * FFT_N1
    band = jnp.linspace(1e-4, HY_BANDS - 1, HY_BANDS, dtype=F32)
    freq = jnp.zeros((1, LANES), F32).at[0, 1:1 + HY_BANDS].set(band).at[0, 1 + HY_BANDS:1 + 2 * HY_BANDS].set(band)
    w1p = jnp.zeros((LANES, ffn), F32).at[:w1.shape[0]].set(w1)
    full = lambda shape: pl.BlockSpec(shape, lambda i, p: (0,) * len(shape))
    return pl.pallas_call(
        functools.partial(_filter_kernel, tt=tt, seq=seq, nparts=nparts),
        grid=(seq // tt, nparts),
        in_specs=[full((1, LANES)), full((LANES, ffn)), full((1, ffn)), full((1, ffn)),
                  full((ffn, ffn)), full((1, ffn)), full((1, ffn)),
                  pl.BlockSpec((ffn, c), lambda i, p: (0, p)), pl.BlockSpec((1, c), lambda i, p: (0, p))],
        out_specs=pl.BlockSpec((1, SUBLANES, FFT_N1 * c), lambda i, p: (p, i, 0)),
        out_shape=jax.ShapeDtypeStruct((nparts, seq // FFT_N1, FFT_N1 * c), F32),
        scratch_shapes=[pltpu.VMEM((c // LANES, tt, LANES), F32), pltpu.VMEM((tt, ffn), F32)],
        compiler_params=_cparams(2),
        name="hyena_filters",
    )(freq, w1p, b1.reshape(1, ffn), f1.reshape(1, ffn), w2, b2.reshape(1, ffn), f2.reshape(1, ffn),
      w3, decay.reshape(1, -1))


def _fft_tables(seq):
    n2h = seq // FFT_N1
    n2 = 2 * n2h
    n = FFT_N1 * n2
    k2n = n2h + 1
    k2p = _round_up(k2n, BF16_ROWS)
    kk = np.arange(k2p)[:, None]
    nn = np.arange(n2h)[None, :]
    ang = 2.0 * np.pi * ((kk * nn) % n2) / n2
    valid = (kk < k2n).astype(np.float64)
    fa = np.concatenate([np.cos(ang) * valid, -np.sin(ang) * valid], axis=0)
    wk = np.where((kk == 0) | (kk == n2h), 1.0, 2.0) * valid
    gc = np.concatenate([(np.cos(ang) * wk).T, (-np.sin(ang) * wk).T], axis=1)
    n1 = np.arange(FFT_N1)[None, :]
    tw_ang = 2.0 * np.pi * ((kk * n1) % n) / n
    twr = (np.cos(tw_ang) * valid).reshape(k2p, 1, FFT_N1)
    twi = (-np.sin(tw_ang) * valid).reshape(k2p, 1, FFT_N1)
    k1 = np.arange(FFT_N1)[:, None]
    base = 2.0 * np.pi * ((k1 * n1) % FFT_N1) / FFT_N1
    return dict(n2h=n2h, k2p=k2p, n=n,
                fa=jnp.asarray(fa, BF16), gc=jnp.asarray(gc, BF16),
                twr=jnp.asarray(twr, F32), twi=jnp.asarray(twi, F32),
                cr=jnp.asarray(np.cos(base), F32), ci=jnp.asarray(-np.sin(base), F32))


def _pack_pair(re, im):
    hi = lax.bitcast_convert_type(re.astype(BF16).astype(F32), U32)
    lo = lax.bitcast_convert_type(im.astype(BF16).astype(F32), U32)
    return hi | (lo >> 16)


def _unpack_pair(w):
    re = lax.bitcast_convert_type(w & jnp.uint32(0xFFFF0000), F32)
    im = lax.bitcast_convert_type(w << 16, F32)
    return re, im


def _ffta_kernel(x_ref, fa_ref, o_ref, *, k2p, c):
    for j in range(SUBLANES):
        xs = x_ref[0, :, j * c:(j + 1) * c].astype(BF16)
        res = jnp.dot(fa_ref[...], xs, preferred_element_type=F32)
        w = _pack_pair(res[:k2p], res[k2p:])
        for ct in range(c // LANES):
            o_ref[0, 0, ct, pl.ds(j, k2p, stride=SUBLANES), :] = w[:, ct * LANES:(ct + 1) * LANES]


def _fft_stage_a(xf, fa, c):
    bx, n2h, _ = xf.shape
    k2p = fa.shape[0] // 2
    ngrp = FFT_N1 // SUBLANES
    nct = c // LANES
    return pl.pallas_call(
        functools.partial(_ffta_kernel, k2p=k2p, c=c),
        grid=(bx, ngrp),
        in_specs=[pl.BlockSpec((1, n2h, SUBLANES * c), lambda b, g: (b, 0, g)),
                  pl.BlockSpec((2 * k2p, n2h), lambda b, g: (0, 0))],
        out_specs=pl.BlockSpec((1, 1, nct, k2p * SUBLANES, LANES), lambda b, g: (b, g, 0, 0, 0)),
        out_shape=jax.ShapeDtypeStruct((bx, ngrp, nct, k2p * SUBLANES, LANES), U32),
        compiler_params=_cparams(2),
        name="fft_stage_a",
    )(xf, fa)


def _load_stage_b(a_ref, b):
    ngrp, nct = a_ref.shape[1], a_ref.shape[2]
    return jnp.concatenate([jnp.concatenate([a_ref[b, g, ct] for ct in range(nct)], axis=1)
                            for g in range(ngrp)], axis=0)


def _stage_b_matrix(twr_ref, twi_ref, cr_ref, ci_ref):
    tr = twr_ref[0]
    ti = twi_ref[0]
    cr = cr_ref[...]
    ci = ci_ref[...]
    mr = cr * tr - ci * ti
    mi = ci * tr + cr * ti
    return mr, mi


def _fftb_filter_kernel(a_ref, twr_ref, twi_ref, cr_ref, ci_ref, o_ref, *, scale):
    mr, mi = _stage_b_matrix(twr_ref, twi_ref, cr_ref, ci_ref)
    fb = jnp.concatenate([jnp.concatenate([mr, -mi], axis=1),
                          jnp.concatenate([mi, mr], axis=1)], axis=0).astype(BF16)
    nparts = a_ref.shape[0]
    c = a_ref.shape[2] * LANES
    hs = []
    for p in range(nparts):
        ar, ai = _unpack_pair(_load_stage_b(a_ref, p))
        a = jnp.concatenate([ar, ai], axis=0).astype(BF16)
        hs.append(jnp.dot(fb, a, preferred_element_type=F32))
    for order in range(nparts // 2):
        fwd, bwd = hs[order], hs[nparts // 2 + order]
        o_ref[0, 0, :, order * c:(order + 1) * c] = (fwd[:FFT_N1] + bwd[:FFT_N1]) * scale
        o_ref[0, 1, :, order * c:(order + 1) * c] = (fwd[FFT_N1:] - bwd[FFT_N1:]) * scale


def _fft_stage_b_filter(ap, tabs, scale):
    k2p = tabs["k2p"]
    nparts, ngrp, nct = ap.shape[:3]
    c = nct * LANES
    return pl.pallas_call(
        functools.partial(_fftb_filter_kernel, scale=scale),
        grid=(k2p,),
        in_specs=[pl.BlockSpec((nparts, ngrp, nct, SUBLANES, LANES), lambda k: (0, 0, 0, k, 0)),
                  pl.BlockSpec((1, 1, FFT_N1), lambda k: (k, 0, 0)),
                  pl.BlockSpec((1, 1, FFT_N1), lambda k: (k, 0, 0)),
                  pl.BlockSpec((FFT_N1, FFT_N1), lambda k: (0, 0)),
                  pl.BlockSpec((FFT_N1, FFT_N1), lambda k: (0, 0))],
        out_specs=pl.BlockSpec((1, 2, FFT_N1, nparts // 2 * c), lambda k: (k, 0, 0, 0)),
        out_shape=jax.ShapeDtypeStruct((k2p, 2, FFT_N1, nparts // 2 * c), F32),
        compiler_params=_cparams(1),
        name="fft_stage_b_filter",
    )(ap, tabs["twr"], tabs["twi"], tabs["cr"], tabs["ci"])


def _fftb_conv_kernel(a_ref, g_ref, twr_ref, twi_ref, cr_ref, ci_ref, o_ref):
    mr, mi = _stage_b_matrix(twr_ref, twi_ref, cr_ref, ci_ref)
    fb = jnp.concatenate([jnp.concatenate([mr, -mi], axis=1),
                          jnp.concatenate([mi, mr], axis=1)], axis=0).astype(BF16)
    mrt, mit = mr.T, mi.T
    fbt = jnp.concatenate([jnp.concatenate([mrt, mit], axis=1),
                           jnp.concatenate([-mit, mrt], axis=1)], axis=0).astype(BF16)
    gr = g_ref[0, 0]
    gi = g_ref[0, 1]
    ngrp, nct = a_ref.shape[1], a_ref.shape[2]
    for b in range(a_ref.shape[0]):
        ar, ai = _unpack_pair(_load_stage_b(a_ref, b))
        a = jnp.concatenate([ar, ai], axis=0).astype(BF16)
        xh = jnp.dot(fb, a, preferred_element_type=F32)
        xr, xi = xh[:FFT_N1], xh[FFT_N1:]
        y = jnp.concatenate([xr * gr - xi * gi, xr * gi + xi * gr], axis=0).astype(BF16)
        cc = jnp.dot(fbt, y, preferred_element_type=F32)
        w = _pack_pair(cc[:FFT_N1], cc[FFT_N1:])
        for g in range(ngrp):
            for ct in range(nct):
                o_ref[b, g, ct] = w[g * SUBLANES:(g + 1) * SUBLANES, ct * LANES:(ct + 1) * LANES]


def _fft_stage_b_conv(ap, g, order, tabs):
    bsz, ngrp, nct = ap.shape[:3]
    k2p = tabs["k2p"]
    c = nct * LANES
    blk = pl.BlockSpec((bsz, ngrp, nct, SUBLANES, LANES), lambda k: (0, 0, 0, k, 0))
    return pl.pallas_call(
        _fftb_conv_kernel,
        grid=(k2p,),
        in_specs=[blk,
                  pl.BlockSpec((1, 2, FFT_N1, c), lambda k: (k, 0, 0, order)),
                  pl.BlockSpec((1, 1, FFT_N1), lambda k: (k, 0, 0)),
                  pl.BlockSpec((1, 1, FFT_N1), lambda k: (k, 0, 0)),
                  pl.BlockSpec((FFT_N1, FFT_N1), lambda k: (0, 0)),
                  pl.BlockSpec((FFT_N1, FFT_N1), lambda k: (0, 0))],
        out_specs=blk,
        out_shape=jax.ShapeDtypeStruct(ap.shape, U32),
        compiler_params=_cparams(1),
        name="fft_stage_b_conv",
    )(ap, g, tabs["twr"], tabs["twi"], tabs["cr"], tabs["ci"])


def _fftc_kernel(c_ref, gc_ref, u_ref, gate_ref, d_ref, o_ref, *, k2p, c):
    for j in range(SUBLANES):
        w = jnp.concatenate([c_ref[0, 0, ct, pl.ds(j, k2p, stride=SUBLANES), :] for ct in range(c // LANES)],
                            axis=1)
        re, im = _unpack_pair(w)
        cm = jnp.concatenate([re, im], axis=0).astype(BF16)
        y = jnp.dot(gc_ref[...], cm, preferred_element_type=F32)
        sl = slice(j * c, (j + 1) * c)
        o_ref[0, :, sl] = gate_ref[0, :, sl] * (y + d_ref[...] * u_ref[0, :, sl])


def _fft_stage_c(cp, gc, uf, gatef, d_row):
    bsz, ngrp, nct = cp.shape[:3]
    n2h = uf.shape[1]
    c = nct * LANES
    k2p = gc.shape[1] // 2
    fblk = pl.BlockSpec((1, n2h, SUBLANES * c), lambda b, g: (b, 0, g))
    return pl.pallas_call(
        functools.partial(_fftc_kernel, k2p=k2p, c=c),
        grid=(bsz, ngrp),
        in_specs=[pl.BlockSpec((1, 1, nct, k2p * SUBLANES, LANES), lambda b, g: (b, g, 0, 0, 0)),
                  pl.BlockSpec((n2h, 2 * k2p), lambda b, g: (0, 0)),
                  fblk, fblk,
                  pl.BlockSpec((1, c), lambda b, g: (0, 0))],
        out_specs=fblk,
        out_shape=jax.ShapeDtypeStruct(uf.shape, F32),
        compiler_params=_cparams(2),
        name="fft_stage_c",
    )(cp, gc, uf, gatef, d_row)


def _hyena_mix(hv, hx1, hx2, filt, skip, seq):
    c = skip.shape[1]
    tabs = _fft_tables(seq)
    g = _fft_stage_b_filter(_fft_stage_a(filt, tabs["fa"], c), tabs, 1.0 / tabs["n"])

    def conv(u, gate, order):
        cp = _fft_stage_b_conv(_fft_stage_a(u, tabs["fa"], c), g, order, tabs)
        return _fft_stage_c(cp, tabs["gc"], u, gate, skip[order].reshape(1, c))

    return conv(conv(hv, hx1, 0), hx2, 1)


def _qkprep_kernel(q_ref, k_ref, v_ref, qn_ref, kn_ref, oqt_ref, ok_ref, ovt_ref, *, tt, seq):
    i = pl.program_id(0)
    t = (i * tt + lax.broadcasted_iota(I32, (tt, HEAD_DIM), 0)) % seq
    lane = lax.broadcasted_iota(I32, (tt, HEAD_DIM), 1)
    j = lane % (HEAD_DIM // 2)
    quarter = HEAD_DIM // 4
    f = (j % quarter).astype(F32)
    inv = jnp.exp(f * (-2.0 * math.log(ROPE_THETA) / (HEAD_DIM // 2)))
    pos = jnp.where(j < quarter, t // GRID_W, t % GRID_W).astype(F32)
    ang = pos * inv
    cosv = jnp.cos(ang)
    sinv = jnp.sin(ang)
    sin_signed = jnp.where(lane < HEAD_DIM // 2, -sinv, sinv)

    def norm_rope(x, g, scale):
        ms = jnp.mean(x * x, axis=-1, keepdims=True)
        xn = x * lax.rsqrt(ms + NORM_EPS) * g
        return (xn * cosv + pltpu.roll(xn, HEAD_DIM // 2, axis=1) * sin_signed) * scale

    qg = qn_ref[...]
    kg = kn_ref[...]
    q_scale = LOG2_E * HEAD_DIM ** -0.5
    for h in range(N_HEADS):
        sl = slice(h * HEAD_DIM, (h + 1) * HEAD_DIM)
        oqt_ref[h] = norm_rope(q_ref[:, sl], qg, q_scale).T.astype(BF16)
    for h in range(N_KV_HEADS):
        sl = slice(h * HEAD_DIM, (h + 1) * HEAD_DIM)
        ok_ref[:, sl] = norm_rope(k_ref[:, sl], kg, 1.0).astype(BF16)
        ovt_ref[h, :HEAD_DIM, :] = v_ref[:, sl].T.astype(BF16)
        extra = lax.broadcasted_iota(I32, (V_ROWS - HEAD_DIM, tt), 0)
        ovt_ref[h, HEAD_DIM:, :] = jnp.where(extra == 0, 1.0, 0.0).astype(BF16)


def _qk_prep(proj, qn, kn, seq):
    n = proj.shape[0]
    tt = min(512, seq)
    qw = N_HEADS * HEAD_DIM
    kw = N_KV_HEADS * HEAD_DIM
    q_off = (3 * HY_WIDTH) // qw
    k_off = (3 * HY_WIDTH + qw) // kw
    return pl.pallas_call(
        functools.partial(_qkprep_kernel, tt=tt, seq=seq),
        grid=(n // tt,),
        in_specs=[pl.BlockSpec((tt, qw), lambda i: (i, q_off)),
                  pl.BlockSpec((tt, kw), lambda i: (i, k_off)),
                  pl.BlockSpec((tt, kw), lambda i: (i, k_off + 1)),
                  pl.BlockSpec((1, HEAD_DIM), lambda i: (0, 0)),
                  pl.BlockSpec((1, HEAD_DIM), lambda i: (0, 0))],
        out_specs=[pl.BlockSpec((N_HEADS, HEAD_DIM, tt), lambda i: (0, 0, i)),
                   pl.BlockSpec((tt, kw), lambda i: (i, 0)),
                   pl.BlockSpec((N_KV_HEADS, V_ROWS, tt), lambda i: (0, 0, i))],
        out_shape=[jax.ShapeDtypeStruct((N_HEADS, HEAD_DIM, n), BF16),
                   jax.ShapeDtypeStruct((n, kw), BF16),
                   jax.ShapeDtypeStruct((N_KV_HEADS, V_ROWS, n), BF16)],
        compiler_params=_cparams(1),
        name="qk_prep",
    )(proj, proj, proj, qn, kn)


def _flash_kernel(qt_ref, k_ref, vt_ref, o_ref, acc_sc, s_sc, mc_sc, *, tq, tk, seq, unroll):
    nq = Q_PER_KV * tq
    nk = seq // tk
    qt = jnp.concatenate([qt_ref[g] for g in range(Q_PER_KV)], axis=1)
    acc_sc[...] = jnp.zeros(acc_sc.shape, F32)

    def scores(c, slot):
        start = pl.multiple_of(c * tk, tk)
        s = jnp.dot(k_ref[pl.ds(start, tk), :], qt, preferred_element_type=F32)
        s_sc[slot] = s
        mc_sc[slot] = jnp.max(s, axis=0, keepdims=True)

    scores(0, 0)

    def trip(i, m, last):
        for u in range(unroll):
            c = i * unroll + u
            if not (last and u == unroll - 1):
                scores(c + 1, (u + 1) % 2)
            s = s_sc[u % 2]
            mc = mc_sc[u % 2]
            start = pl.multiple_of(c * tk, tk)
            vc = vt_ref[0, :, pl.ds(start, tk)]
            p = jnp.exp2(s - mc).astype(BF16)
            oc = jnp.dot(vc, p, preferred_element_type=F32)
            m_new = jnp.maximum(m, mc)
            acc_sc[...] = acc_sc[...] * jnp.exp2(m - m_new) + oc * jnp.exp2(mc - m_new)
            m = m_new
        return m

    ntrips = nk // unroll
    m = lax.fori_loop(0, ntrips - 1, lambda i, m: trip(i, m, False), jnp.full((1, nq), -jnp.inf, F32))
    trip(ntrips - 1, m, True)
    o = acc_sc[:HEAD_DIM, :] / acc_sc[HEAD_DIM:HEAD_DIM + 1, :]
    for g in range(Q_PER_KV):
        o_ref[:, g * HEAD_DIM:(g + 1) * HEAD_DIM] = o[:, g * tq:(g + 1) * tq].T


def _flash_attention(qt, k, vt, bsz, seq):
    n = bsz * seq
    tq = min(256, seq)
    tk = min(1024, seq // 2)
    unroll = 4 if seq // tk >= 8 else (2 if seq // tk >= 2 else 1)
    gw = Q_PER_KV * HEAD_DIM
    nqb = seq // tq
    return pl.pallas_call(
        functools.partial(_flash_kernel, tq=tq, tk=tk, seq=seq, unroll=unroll),
        grid=(bsz, N_KV_HEADS, nqb),
        in_specs=[pl.BlockSpec((Q_PER_KV, HEAD_DIM, tq), lambda b, h, i: (h, 0, b * nqb + i)),
                  pl.BlockSpec((seq, HEAD_DIM), lambda b, h, i: (b, h)),
                  pl.BlockSpec((1, V_ROWS, seq), lambda b, h, i: (h, 0, b))],
        out_specs=pl.BlockSpec((tq, gw), lambda b, h, i: (b * nqb + i, h)),
        out_shape=jax.ShapeDtypeStruct((n, N_HEADS * HEAD_DIM), F32),
        scratch_shapes=[pltpu.VMEM((V_ROWS, Q_PER_KV * tq), F32),
                        pltpu.VMEM((2, tk, Q_PER_KV * tq), F32),
                        pltpu.VMEM((2, 1, Q_PER_KV * tq), F32)],
        compiler_params=_cparams(3),
        name="flash_attention",
    )(qt, k, vt)


def _layer_norm(y, g, b):
    mu = jnp.mean(y, axis=-1, keepdims=True)
    yc = y - mu
    var = jnp.mean(yc * yc, axis=-1, keepdims=True)
    return yc * lax.rsqrt(var + NORM_EPS) * g + b


def _merge_kernel(yh_ref, ya_ref, x_ref, gh_ref, ga_ref, wo_ref, lg_ref, lb_ref, wr_ref, br_ref,
                  x1_ref, aff_ref):
    hw = yh_ref.shape[1]
    yh = yh_ref[...]
    ya = ya_ref[...]
    mh = yh * lax.rsqrt(jnp.mean(yh * yh, axis=-1, keepdims=True) + NORM_EPS) * gh_ref[...]
    ma = ya * lax.rsqrt(jnp.mean(ya * ya, axis=-1, keepdims=True) + NORM_EPS) * ga_ref[...]
    y = (jnp.dot(mh.astype(BF16), wo_ref[:hw, :], preferred_element_type=F32)
         + jnp.dot(ma.astype(BF16), wo_ref[hw:, :], preferred_element_type=F32)
         + DN_ALPHA * x_ref[...])
    x1 = _layer_norm(y, lg_ref[...], lb_ref[...])
    x1_ref[...] = x1
    x_hi = x1.astype(BF16)
    x_lo = (x1 - x_hi.astype(F32)).astype(BF16)
    both = jnp.dot(x_hi, wr_ref[...], preferred_element_type=F32)
    logits = (both[:, :LANES] + both[:, LANES:]
              + jnp.dot(x_lo, wr_ref[:, :LANES], preferred_element_type=F32) + br_ref[...])
    e = jnp.exp(logits - jnp.max(logits, axis=-1, keepdims=True))
    aff_ref[...] = e / jnp.sum(e, axis=-1, keepdims=True)


def _merge(y_hy, y_att, x2, g_hy, g_attn, wo_bf, ln_g, ln_b, wr_pad, br_pad):
    n, d = x2.shape
    hw = y_hy.shape[1]
    tm = min(256, n)
    row = lambda w: pl.BlockSpec((1, w), lambda i: (0, 0))
    return pl.pallas_call(
        _merge_kernel,
        grid=(n // tm,),
        in_specs=[pl.BlockSpec((tm, hw), lambda i: (i, 0)),
                  pl.BlockSpec((tm, d - hw), lambda i: (i, 0)),
                  pl.BlockSpec((tm, d), lambda i: (i, 0)),
                  row(hw), row(d - hw),
                  pl.BlockSpec((d, d), lambda i: (0, 0)),
                  row(d), row(d),
                  pl.BlockSpec((d, 2 * LANES), lambda i: (0, 0)),
                  row(LANES)],
        out_specs=[pl.BlockSpec((tm, d), lambda i: (i, 0)),
                   pl.BlockSpec((tm, LANES), lambda i: (i, 0))],
        out_shape=[jax.ShapeDtypeStruct((n, d), F32), jax.ShapeDtypeStruct((n, LANES), F32)],
        compiler_params=_cparams(1),
        name="merge_ln_router",
    )(y_hy, y_att, x2, g_hy, g_attn, wo_bf, ln_g, ln_b, wr_pad, br_pad)


def _select_kernel(a_ref, idx_ref, gslot_ref, pos_ref, excl_ref, *, cap):
    a = a_ref[0]
    ne, nr, _ = a.shape
    bits = lax.bitcast_convert_type(a, I32)

    def count(mask):
        return jnp.sum(jnp.sum(mask.astype(F32), axis=2, keepdims=True), axis=1, keepdims=True)

    thr = jnp.zeros((ne, 1, 1), I32)
    for bit in range(30, -1, -1):
        cand = thr | (1 << bit)
        thr = jnp.where(count(bits >= cand) >= cap, cand, thr)

    ii = lax.broadcasted_iota(I32, (LANES, LANES), 0)
    jj = lax.broadcasted_iota(I32, (LANES, LANES), 1)
    upper = (ii <= jj).astype(BF16)
    ri = lax.broadcasted_iota(I32, (ne, nr, nr), 1)
    rj = lax.broadcasted_iota(I32, (ne, nr, nr), 2)
    lower_strict = (rj < ri).astype(BF16)

    def prefix(mask):
        mb = mask.astype(BF16).reshape(ne * nr, LANES)
        pin = jnp.dot(mb, upper, preferred_element_type=F32).reshape(ne, nr, LANES)
        tot = pin[:, :, LANES - 1:LANES]
        totb = jnp.broadcast_to(tot, (ne, nr, LANES)).astype(BF16)
        off = jnp.einsum("ers,esl->erl", lower_strict, totb, preferred_element_type=F32)
        return pin, tot, off[:, :, 0:1]

    gt = bits > thr
    eq = bits == thr
    need = cap - count(gt)
    pin_eq, _, off_eq = prefix(eq)
    eq_rank = pin_eq + off_eq - eq.astype(F32)
    chosen = gt | (eq & (eq_rank < need))
    pin, tot, off = prefix(chosen)
    chosen_f = chosen.astype(F32)
    excl = pin + off - chosen_f
    excl_ref[0] = excl
    pos_ref[0] = jnp.where(chosen, excl, -1.0)

    s_row = lax.broadcasted_iota(I32, (nr, cap), 1).astype(F32)
    r_col = lax.broadcasted_iota(I32, (nr, cap), 0).astype(F32)
    j_col = lax.broadcasted_iota(I32, (LANES, cap), 0).astype(F32)
    for e in range(ne):
        lo = off[e]
        hi = lo + tot[e]
        in_row = ((lo <= s_row) & (s_row < hi)).astype(F32)
        in_row_b = in_row.astype(BF16)
        row_of_s = jnp.sum(in_row * r_col, axis=0, keepdims=True)
        s_local = s_row[0:1] - jnp.sum(in_row * lo, axis=0, keepdims=True)
        pin_t = pin[e].T.astype(BF16)
        g = jnp.dot(pin_t, in_row_b, preferred_element_type=F32)
        lane_of_s = jnp.sum((g <= s_local).astype(F32), axis=0, keepdims=True)
        idx_ref[0, e:e + 1, :] = (row_of_s * LANES + lane_of_s).astype(I32)
        rem = a[e]
        a_of_s = jnp.zeros((LANES, cap), F32)
        for _ in range(3):
            part = rem.astype(BF16)
            rem = rem - part.astype(F32)
            a_of_s = a_of_s + jnp.dot(part.astype(F32).T.astype(BF16), in_row_b, preferred_element_type=F32)
        gslot_ref[0, e:e + 1, :] = jnp.sum(jnp.where(j_col == lane_of_s, a_of_s, 0.0), axis=0, keepdims=True)


def _select(aff4, cap):
    bsz, ne, nr, _ = aff4.shape
    blk = pl.BlockSpec((1, ne, nr, LANES), lambda b: (b, 0, 0, 0))
    slot = pl.BlockSpec((1, ne, cap), lambda b: (b, 0, 0))
    return pl.pallas_call(
        functools.partial(_select_kernel, cap=cap),
        grid=(bsz,),
        in_specs=[blk],
        out_specs=[slot, slot, blk, blk],
        out_shape=[jax.ShapeDtypeStruct((bsz, ne, cap), I32),
                   jax.ShapeDtypeStruct((bsz, ne, cap), F32),
                   jax.ShapeDtypeStruct(aff4.shape, F32),
                   jax.ShapeDtypeStruct(aff4.shape, F32)],
        compiler_params=_cparams(1),
        name="expert_choice_select",
    )(aff4)


def _expert_kernel(idx_ref, idxn_ref, x_hbm, g_ref, wg_ref, wu_ref, wd_ref, o_ref, xg_a, xg_b, sem,
                   *, tc, nchunk, seq, ff, npair, bsz, nsteps):
    s = pl.program_id(0)

    def issue(ids_ref, first, step, buf, k):
        base = ((step // npair) % bsz) * seq
        for r in range(tc):
            pltpu.make_async_copy(x_hbm.at[pl.ds(base + ids_ref[0, 0, first + r], 1)],
                                  buf.at[pl.ds(r, 1)], sem.at[k]).start()

    def wait_rows(buf, k):
        pltpu.make_async_copy(x_hbm.at[pl.ds(0, tc)], buf, sem.at[k]).wait()

    def ffn(buf, first):
        xb = buf[...].astype(BF16)
        gate = jnp.dot(xb, wg_ref[0], preferred_element_type=F32)
        h = gate * jax.nn.sigmoid(gate) * jnp.dot(xb, wu_ref[0], preferred_element_type=F32)
        y = jnp.dot(h.astype(BF16), wd_ref[0], preferred_element_type=F32)
        o_ref[0, 0, first:first + tc, :] = (y * g_ref[0, first:first + tc, :]).astype(BF16)

    @pl.when(s == 0)
    def _():
        issue(idx_ref, 0, s, xg_a, 0)

    bufs = (xg_a, xg_b)
    for k in range(nchunk):
        wait_rows(bufs[k % 2], k % 2)
        if k + 1 < nchunk:
            issue(idx_ref, (k + 1) * tc, s, bufs[(k + 1) % 2], (k + 1) % 2)
        else:
            issue(idxn_ref, 0, jnp.minimum(s + 1, nsteps - 1), xg_a, 0)
        ffn(bufs[k % 2], k * tc)

    @pl.when(s == nsteps - 1)
    def _():
        wait_rows(xg_a, 0)


def _experts(idx, gslot, x1, wg_bf, wu_bf, wd_bf, bsz, seq):
    _, ne, cap = idx.shape
    d = x1.shape[1]
    ff = wd_bf.shape[1]
    tc = min(256, cap // 2)
    nchunk = 4 if cap % (4 * tc) == 0 else 2
    rows = nchunk * tc
    npair = cap // rows
    nsteps = ne * bsz * npair
    idx_s = idx.transpose(1, 0, 2).reshape(nsteps, 1, rows)
    g_s = gslot.transpose(1, 0, 2).reshape(nsteps, rows, 1)
    out_map = lambda s: ((s // npair) % bsz, s // (bsz * npair), s % npair, 0)
    wmap = lambda s: (s // (bsz * npair), 0, 0)
    return pl.pallas_call(
        functools.partial(_expert_kernel, tc=tc, nchunk=nchunk, seq=seq, ff=ff, npair=npair, bsz=bsz,
                          nsteps=nsteps),
        grid=(nsteps,),
        in_specs=[pl.BlockSpec((1, 1, rows), lambda s: (s, 0, 0), memory_space=pltpu.SMEM),
                  pl.BlockSpec((1, 1, rows), lambda s: (jnp.minimum(s + 1, nsteps - 1), 0, 0),
                               memory_space=pltpu.SMEM),
                  pl.BlockSpec(memory_space=pl.ANY),
                  pl.BlockSpec((1, rows, 1), lambda s: (s, 0, 0)),
                  pl.BlockSpec((1, d, ff), wmap),
                  pl.BlockSpec((1, d, ff), wmap),
                  pl.BlockSpec((1, ff, d), wmap)],
        out_specs=pl.BlockSpec((1, 1, rows, d), out_map),
        out_shape=jax.ShapeDtypeStruct((bsz, ne, cap, d), BF16),
        scratch_shapes=[pltpu.VMEM((tc, d), F32), pltpu.VMEM((tc, d), F32), pltpu.SemaphoreType.DMA((2,))],
        compiler_params=_cparams(1),
        name="expert_ffn",
    )(idx_s, idx_s, x1, g_s, wg_bf, wu_bf, wd_bf)


def _combine_kernel(w0_ref, y_hbm, x1_ref, pos_ref, lg_ref, lb_ref, o_ref, ybuf, xbuf, acc_ref, sem, xsem,
                    *, tb_tokens, win, cap, ne, ntb):
    s = pl.program_id(0)
    nsteps = pl.num_programs(0)
    slot = s % 2

    def window_copies(step, to_slot):
        bb = step // ntb
        wbase = (bb * (ntb + 1) + step % ntb) * ne
        copies, firsts = [], []
        for e in range(ne):
            sa = jnp.minimum((w0_ref[wbase + e] // BF16_ROWS) * BF16_ROWS, cap - win)
            sa = pl.multiple_of(sa, BF16_ROWS)
            firsts.append(sa)
            copies.append(pltpu.make_async_copy(y_hbm.at[bb, e, pl.ds(sa, win)],
                                                ybuf.at[to_slot, pl.ds(e * win, win)], sem.at[to_slot, e]))
        return copies, firsts

    @pl.when(s == 0)
    def _():
        for cp in window_copies(s, 0)[0]:
            cp.start()

    @pl.when(s + 1 < nsteps)
    def _():
        for cp in window_copies(s + 1, 1 - slot)[0]:
            cp.start()

    b = s // ntb
    base = (b * (ntb + 1) + s % ntb) * ne
    copies, starts = window_copies(s, slot)
    lane = lax.broadcasted_iota(I32, (tb_tokens, win), 1).astype(F32)
    onehots = [(pos_ref[0, :, e:e + 1] - starts[e].astype(F32) == lane).astype(BF16) for e in range(ne)]
    p = jnp.concatenate(onehots, axis=1)
    for cp in copies:
        cp.wait()
    acc_ref[...] = DN_ALPHA * x1_ref[...] + jnp.dot(p, ybuf[slot], preferred_element_type=F32)

    for e in range(ne):
        first_end = starts[e] + win
        n_extra = jnp.maximum(w0_ref[base + ne + e] - first_end + win - 1, 0) // win

        def extra(w, carry, e=e, first_end=first_end):
            nominal = first_end + w * win
            st = pl.multiple_of(jnp.minimum(nominal, cap - win), BF16_ROWS)
            cp = pltpu.make_async_copy(y_hbm.at[b, e, pl.ds(st, win)], xbuf, xsem)
            cp.start()
            cp.wait()
            pe = pos_ref[0, :, e:e + 1]
            oh = ((pe - st.astype(F32) == lane) & (pe >= nominal.astype(F32))).astype(BF16)
            acc_ref[...] += jnp.dot(oh, xbuf[...], preferred_element_type=F32)
            return carry

        lax.fori_loop(0, n_extra, extra, 0)
    o_ref[...] = _layer_norm(acc_ref[...], lg_ref[...], lb_ref[...])


def _combine(y, x1, pos_t, w0, ln_g, ln_b, bsz, seq, tbt):
    _, ne, cap, d = y.shape
    win = min(LANES, cap)
    ntb = seq // tbt
    grid_spec = pltpu.PrefetchScalarGridSpec(
        num_scalar_prefetch=1,
        grid=(bsz * ntb,),
        in_specs=[pl.BlockSpec(memory_space=pl.ANY),
                  pl.BlockSpec((tbt, d), lambda s, w: (s, 0)),
                  pl.BlockSpec((1, tbt, ne), lambda s, w: (s // ntb, s % ntb, 0)),
                  pl.BlockSpec((1, d), lambda s, w: (0, 0)),
                  pl.BlockSpec((1, d), lambda s, w: (0, 0))],
        out_specs=pl.BlockSpec((tbt, d), lambda s, w: (s, 0)),
        scratch_shapes=[pltpu.VMEM((2, ne * win, d), BF16), pltpu.VMEM((win, d), BF16),
                        pltpu.VMEM((tbt, d), F32),
                        pltpu.SemaphoreType.DMA((2, ne)), pltpu.SemaphoreType.DMA(())],
    )
    return pl.pallas_call(
        functools.partial(_combine_kernel, tb_tokens=tbt, win=win, cap=cap, ne=ne, ntb=ntb),
        grid_spec=grid_spec,
        out_shape=jax.ShapeDtypeStruct((bsz * seq, d), F32),
        compiler_params=_cparams(1),
        name="moe_combine_ln",
    )(w0, y, x1, pos_t, ln_g, ln_b)


def _deinterleave(width):
    idx = np.arange(width).reshape(-1, HEAD_DIM // 2, 2)
    return np.concatenate([idx[:, :, 0], idx[:, :, 1]], axis=1).reshape(-1)


def kernel(x, w_in, b_in, hy_conv_w, hy_conv_b, hy_ffn_w1, hy_ffn_b1, hy_sin_f1, hy_ffn_w2, hy_ffn_b2, hy_sin_f2, hy_ffn_w3, hy_decay, hy_skip, q_norm, k_norm, g_hy, g_attn, w_out, ln1_g, ln1_b, w_router, b_router, w_gate, w_up, w_down, ln2_g, ln2_b):
    bsz, seq, d = x.shape
    n = bsz * seq
    x2 = x.reshape(n, d)
    s1 = 3 * HY_WIDTH
    qw = N_HEADS * HEAD_DIM
    kw = N_KV_HEADS * HEAD_DIM
    perm = np.concatenate([np.arange(s1), s1 + _deinterleave(qw), s1 + qw + _deinterleave(kw),
                           np.arange(s1 + qw + kw, w_in.shape[1])])
    w_in_bf = w_in[:, perm].astype(BF16)
    b_in_p = b_in[perm].reshape(1, -1)
    hd_perm = _deinterleave(HEAD_DIM)

    proj = _in_projection(x2, w_in_bf, b_in_p)

    conv_b = hy_conv_b.reshape(1, -1)
    hv, hx1, hx2 = [_short_conv(proj, hy_conv_w, conv_b, part, bsz, seq) for part in range(3)]
    filt = _hyena_filters(seq, hy_ffn_w1, hy_ffn_b1, hy_sin_f1, hy_ffn_w2, hy_ffn_b2, hy_sin_f2,
                          hy_ffn_w3, hy_decay)
    y_hy = _hyena_mix(hv, hx1, hx2, filt, hy_skip, seq).reshape(n, HY_WIDTH)

    qt, kb, vt = _qk_prep(proj, q_norm[hd_perm].reshape(1, HEAD_DIM), k_norm[hd_perm].reshape(1, HEAD_DIM), seq)
    y_att = _flash_attention(qt, kb, vt, bsz, seq)

    ne = w_router.shape[1]
    wr_hi = w_router.astype(BF16)
    wr_lo = (w_router - wr_hi.astype(F32)).astype(BF16)
    wr_pair = (jnp.zeros((d, 2 * LANES), BF16).at[:, :ne].set(wr_hi).at[:, LANES:LANES + ne].set(wr_lo))
    br_pad = jnp.full((1, LANES), -1e30, F32).at[0, :ne].set(b_router)
    x1, aff = _merge(y_hy, y_att, x2, g_hy.reshape(1, -1), g_attn.reshape(1, -1),
                     w_out.astype(BF16), ln1_g.reshape(1, d), ln1_b.reshape(1, d), wr_pair, br_pad)

    cap = CAPACITY_FACTOR * seq // ne
    aff4 = aff[:, :ne].reshape(bsz, seq, ne).transpose(0, 2, 1).reshape(bsz, ne, seq // LANES, LANES)
    idx, gslot, pos, excl = _select(aff4, cap)
    y = _experts(idx, gslot, x1, w_gate.astype(BF16), w_up.astype(BF16), w_down.astype(BF16), bsz, seq)
    pos_t = pos.reshape(bsz, ne, seq).transpose(0, 2, 1)
    tbt = min(256, cap)
    before = excl.reshape(bsz, ne, seq // tbt, tbt)[:, :, :, 0].transpose(0, 2, 1)
    w0 = jnp.concatenate([before, jnp.full((bsz, 1, ne), cap, F32)], axis=1).reshape(-1).astype(I32)
    out = _combine(y, x1, pos_t, w0, ln2_g.reshape(1, d), ln2_b.reshape(1, d), bsz, seq, tbt)
    return out.reshape(bsz, seq, d)
```

```python
import functools
import math

import numpy as np
import jax
import jax.numpy as jnp
from jax import lax
from jax.experimental import pallas as pl
from jax.experimental.pallas import tpu as pltpu

F32 = jnp.float32
BF16 = jnp.bfloat16
I32 = jnp.int32
U32 = jnp.uint32
HIGHEST = lax.Precision.HIGHEST

HY_WIDTH = 1024
HY_BANDS = 16
HEAD_DIM = 128
N_HEADS = 8
N_KV_HEADS = 2
Q_PER_KV = N_HEADS // N_KV_HEADS
ROPE_THETA = 10000.0
GRID_W = 64
N_EXPERTS = 16
CAPACITY_FACTOR = 2
NORM_EPS = 1e-6
DN_ALPHA = 2.0 ** 0.25
LOG2_E = math.log2(math.e)
V_ROWS = HEAD_DIM + 16

LANES = 128
SUBLANES = 8
BF16_ROWS = 16
FFT_N1 = 128
VMEM_LIMIT = 56 * 1024 * 1024


def _cparams(n_axes, vmem=VMEM_LIMIT):
    return pltpu.CompilerParams(dimension_semantics=("arbitrary",) * n_axes, vmem_limit_bytes=vmem)


def _round_up(a, m):
    return (a + m - 1) // m * m


def _inproj_kernel(x_ref, w_ref, b_ref, o_ref, xb_ref):
    @pl.when(pl.program_id(1) == 0)
    def _():
        xb_ref[...] = x_ref[...].astype(BF16)

    o_ref[...] = jnp.dot(xb_ref[...], w_ref[...], preferred_element_type=F32) + b_ref[...]


def _in_projection(x2, w_bf, b_row):
    n, d = x2.shape
    wdt = w_bf.shape[1]
    tm = min(1024, n)
    tn = wdt // 3
    return pl.pallas_call(
        _inproj_kernel,
        grid=(n // tm, wdt // tn),
        in_specs=[pl.BlockSpec((tm, d), lambda i, j: (i, 0)),
                  pl.BlockSpec((d, tn), lambda i, j: (0, j)),
                  pl.BlockSpec((1, tn), lambda i, j: (0, j))],
        out_specs=pl.BlockSpec((tm, tn), lambda i, j: (i, j)),
        out_shape=jax.ShapeDtypeStruct((n, wdt), F32),
        scratch_shapes=[pltpu.VMEM((tm, d), BF16)],
        compiler_params=_cparams(2),
        name="in_projection",
    )(x2, w_bf, b_row)


def _store_fft_layout(res, o_ref, scr):
    tt, c = res.shape
    nct = c // LANES
    for ct in range(nct):
        scr[ct] = res[:, ct * LANES:(ct + 1) * LANES]

    for n1 in range(FFT_N1):
        for ct in range(nct):
            col = n1 * c + ct * LANES
            o_ref[0, :, col:col + LANES] = scr[ct, pl.ds(n1, tt // FFT_N1, stride=FFT_N1), :]


def _shortconv_kernel(m_ref, p_ref, nx_ref, w_ref, b_ref, o_ref, scr, *, tt, seq):
    i = pl.program_id(0)
    r_in = lax.broadcasted_iota(I32, (tt, 1), 0)
    tpos = (i * tt + r_in) % seq
    cur = m_ref[...]
    up = pltpu.roll(cur, 1, axis=0)
    up = jnp.where(r_in == 0, p_ref[SUBLANES - 1:SUBLANES, :], up)
    up = jnp.where(tpos == 0, 0.0, up)
    dn = pltpu.roll(cur, tt - 1, axis=0)
    dn = jnp.where(r_in == tt - 1, nx_ref[0:1, :], dn)
    dn = jnp.where(tpos == seq - 1, 0.0, dn)
    wv = w_ref[...]
    _store_fft_layout(wv[0:1] * up + wv[1:2] * cur + wv[2:3] * dn + b_ref[...], o_ref, scr)


def _short_conv(proj, conv_w, conv_b, part, bsz, seq):
    n = proj.shape[0]
    c = HY_WIDTH
    tt = SUBLANES * FFT_N1
    nrb = n // SUBLANES
    hb = tt // SUBLANES
    tps = seq // tt
    return pl.pallas_call(
        functools.partial(_shortconv_kernel, tt=tt, seq=seq),
        grid=(n // tt,),
        in_specs=[pl.BlockSpec((tt, c), lambda i: (i, part)),
                  pl.BlockSpec((SUBLANES, c), lambda i: (jnp.maximum(i * hb - 1, 0), part)),
                  pl.BlockSpec((SUBLANES, c), lambda i: (jnp.minimum((i + 1) * hb, nrb - 1), part)),
                  pl.BlockSpec((3, c), lambda i: (0, part)),
                  pl.BlockSpec((1, c), lambda i: (0, part))],
        out_specs=pl.BlockSpec((1, SUBLANES, FFT_N1 * c), lambda i: (i // tps, i % tps, 0)),
        out_shape=jax.ShapeDtypeStruct((bsz, seq // FFT_N1, FFT_N1 * c), F32),
        scratch_shapes=[pltpu.VMEM((c // LANES, tt, LANES), F32)],
        compiler_params=_cparams(1),
        name="short_conv",
    )(proj, proj, proj, conv_w, conv_b)


def _filter_kernel(freq_ref, w1_ref, b1_ref, f1_ref, w2_ref, b2_ref, f2_ref, w3_ref, dec_ref, o_ref, h_sc,
                   *, tt, seq, nparts):
    i = pl.program_id(0)
    part = pl.program_id(1)

    def time_of_row(shape):
        rr = lax.broadcasted_iota(I32, shape, 0)
        return i * tt + rr // SUBLANES + FFT_N1 * (rr % SUBLANES)

    @pl.when(part == 0)
    def _():
        r = time_of_row((tt, LANES)).astype(F32)
        lane = lax.broadcasted_iota(I32, (tt, LANES), 1)
        t = r / float(seq - 1)
        w = (2.0 * math.pi / seq) * r
        arg = freq_ref[...] * w
        emb = jnp.where(lane == 0, t,
                        jnp.where(lane <= HY_BANDS, jnp.cos(arg),
                                  jnp.where(lane <= 2 * HY_BANDS, -jnp.sin(arg), 0.0)))
        h = jnp.sin(f1_ref[...] * (jnp.dot(emb, w1_ref[...], preferred_element_type=F32, precision=HIGHEST)
                                   + b1_ref[...]))
        h_sc[...] = jnp.sin(f2_ref[...] * (jnp.dot(h, w2_ref[...], preferred_element_type=F32,
                                                   precision=HIGHEST) + b2_ref[...]))

    hf = h_sc[...]
    wf = w3_ref[...]
    h_hi = hf.astype(BF16)
    h_lo = (hf - h_hi.astype(F32)).astype(BF16)
    w_hi = wf.astype(BF16)
    w_lo = (wf - w_hi.astype(F32)).astype(BF16)
    h3 = (jnp.dot(h_hi, w_hi, preferred_element_type=F32) + jnp.dot(h_lo, w_hi, preferred_element_type=F32)
          + jnp.dot(h_hi, w_lo, preferred_element_type=F32))
    row_t = time_of_row((tt, 1))
    win = jnp.exp(-(row_t.astype(F32) / float(seq - 1)) * jnp.abs(dec_ref[...]))
    out = h3 * win
    out = jnp.where((row_t == 0) & (part >= nparts // 2), 0.0, out)
    c = out.shape[1]
    for n1 in range(FFT_N1):
        o_ref[0, :, n1 * c:(n1 + 1) * c] = out[n1 * SUBLANES:(n1 + 1) * SUBLANES, :]


def _hyena_filters(seq, w1, b1, f1, w2, b2, f2, w3, decay):
    ffn = w1.shape[1]
    c = HY_WIDTH
    nparts = w3.shape[1] // c
    tt = SUBLANES * FFT_N1
    band = jnp.linspace(1e-4, HY_BANDS - 1, HY_BANDS, dtype=F32)
    freq = jnp.zeros((1, LANES), F32).at[0, 1:1 + HY_BANDS].set(band).at[0, 1 + HY_BANDS:1 + 2 * HY_BANDS].set(band)
    w1p = jnp.zeros((LANES, ffn), F32).at[:w1.shape[0]].set(w1)
    full = lambda shape: pl.BlockSpec(shape, lambda i, p: (0,) * len(shape))
    return pl.pallas_call(
        functools.partial(_filter_kernel, tt=tt, seq=seq, nparts=nparts),
        grid=(seq // tt, nparts),
        in_specs=[full((1, LANES)), full((LANES, ffn)), full((1, ffn)), full((1, ffn)),
                  full((ffn, ffn)), full((1, ffn)), full((1, ffn)),
                  pl.BlockSpec((ffn, c), lambda i, p: (0, p)), pl.BlockSpec((1, c), lambda i, p: (0, p))],
        out_specs=pl.BlockSpec((1, SUBLANES, FFT_N1 * c), lambda i, p: (p, i, 0)),
        out_shape=jax.ShapeDtypeStruct((nparts, seq // FFT_N1, FFT_N1 * c), F32),
        scratch_shapes=[pltpu.VMEM((tt, ffn), F32)],
        compiler_params=_cparams(2),
        name="hyena_filters",
    )(freq, w1p, b1.reshape(1, ffn), f1.reshape(1, ffn), w2, b2.reshape(1, ffn), f2.reshape(1, ffn),
      w3, decay.reshape(1, -1))


def _fft_tables(seq):
    n2h = seq // FFT_N1
    n2 = 2 * n2h
    n = FFT_N1 * n2
    k2n = n2h + 1
    k2p = _round_up(k2n, BF16_ROWS)
    kk = np.arange(k2p)[:, None]
    nn = np.arange(n2h)[None, :]
    ang = 2.0 * np.pi * ((kk * nn) % n2) / n2
    valid = (kk < k2n).astype(np.float64)
    fa = np.concatenate([np.cos(ang) * valid, -np.sin(ang) * valid], axis=0)
    wk = np.where((kk == 0) | (kk == n2h), 1.0, 2.0) * valid
    gc = np.concatenate([(np.cos(ang) * wk).T, (-np.sin(ang) * wk).T], axis=1)
    n1 = np.arange(FFT_N1)[None, :]
    tw_ang = 2.0 * np.pi * ((kk * n1) % n) / n
    twr = (np.cos(tw_ang) * valid).reshape(k2p, 1, FFT_N1)
    twi = (-np.sin(tw_ang) * valid).reshape(k2p, 1, FFT_N1)
    k1 = np.arange(FFT_N1)[:, None]
    base = 2.0 * np.pi * ((k1 * n1) % FFT_N1) / FFT_N1
    return dict(n2h=n2h, k2p=k2p, n=n,
                fa=jnp.asarray(fa, BF16), gc=jnp.asarray(gc, BF16),
                twr=jnp.asarray(twr, F32), twi=jnp.asarray(twi, F32),
                cr=jnp.asarray(np.cos(base), F32), ci=jnp.asarray(-np.sin(base), F32))


def _pack_pair(re, im):
    hi = lax.bitcast_convert_type(re.astype(BF16).astype(F32), U32)
    lo = lax.bitcast_convert_type(im.astype(BF16).astype(F32), U32)
    return hi | (lo >> 16)


def _unpack_pair(w):
    re = lax.bitcast_convert_type(w & jnp.uint32(0xFFFF0000), F32)
    im = lax.bitcast_convert_type(w << 16, F32)
    return re, im


def _ffta_kernel(x_ref, fa_ref, o_ref, *, k2p, c):
    for j in range(SUBLANES):
        xs = x_ref[0, :, j * c:(j + 1) * c].astype(BF16)
        res = jnp.dot(fa_ref[...], xs, preferred_element_type=F32)
        w = _pack_pair(res[:k2p], res[k2p:])
        for ct in range(c // LANES):
            o_ref[0, 0, ct, pl.ds(j, k2p, stride=SUBLANES), :] = w[:, ct * LANES:(ct + 1) * LANES]


def _fft_stage_a(xf, fa, c):
    bx, n2h, _ = xf.shape
    k2p = fa.shape[0] // 2
    ngrp = FFT_N1 // SUBLANES
    nct = c // LANES
    return pl.pallas_call(
        functools.partial(_ffta_kernel, k2p=k2p, c=c),
        grid=(bx, ngrp),
        in_specs=[pl.BlockSpec((1, n2h, SUBLANES * c), lambda b, g: (b, 0, g)),
                  pl.BlockSpec((2 * k2p, n2h), lambda b, g: (0, 0))],
        out_specs=pl.BlockSpec((1, 1, nct, k2p * SUBLANES, LANES), lambda b, g: (b, g, 0, 0, 0)),
        out_shape=jax.ShapeDtypeStruct((bx, ngrp, nct, k2p * SUBLANES, LANES), U32),
        compiler_params=_cparams(2),
        name="fft_stage_a",
    )(xf, fa)


def _load_stage_b(a_ref, b):
    ngrp, nct = a_ref.shape[1], a_ref.shape[2]
    return jnp.concatenate([jnp.concatenate([a_ref[b, g, ct] for ct in range(nct)], axis=1)
                            for g in range(ngrp)], axis=0)


def _stage_b_matrix(twr_ref, twi_ref, cr_ref, ci_ref):
    tr = twr_ref[0]
    ti = twi_ref[0]
    cr = cr_ref[...]
    ci = ci_ref[...]
    mr = cr * tr - ci * ti
    mi = ci * tr + cr * ti
    return mr, mi


def _fftb_filter_kernel(a_ref, twr_ref, twi_ref, cr_ref, ci_ref, o_ref, *, scale):
    mr, mi = _stage_b_matrix(twr_ref, twi_ref, cr_ref, ci_ref)
    fb = jnp.concatenate([jnp.concatenate([mr, -mi], axis=1),
                          jnp.concatenate([mi, mr], axis=1)], axis=0).astype(BF16)
    nparts = a_ref.shape[0]
    c = a_ref.shape[2] * LANES
    hs = []
    for p in range(nparts):
        ar, ai = _unpack_pair(_load_stage_b(a_ref, p))
        a = jnp.concatenate([ar, ai], axis=0).astype(BF16)
        hs.append(jnp.dot(fb, a, preferred_element_type=F32))
    for order in range(nparts // 2):
        fwd, bwd = hs[order], hs[nparts // 2 + order]
        o_ref[0, 0, :, order * c:(order + 1) * c] = (fwd[:FFT_N1] + bwd[:FFT_N1]) * scale
        o_ref[0, 1, :, order * c:(order + 1) * c] = (fwd[FFT_N1:] - bwd[FFT_N1:]) * scale


def _fft_stage_b_filter(ap, tabs, scale):
    k2p = tabs["k2p"]
    nparts, ngrp, nct = ap.shape[:3]
    c = nct * LANES
    return pl.pallas_call(
        functools.partial(_fftb_filter_kernel, scale=scale),
        grid=(k2p,),
        in_specs=[pl.BlockSpec((nparts, ngrp, nct, SUBLANES, LANES), lambda k: (0, 0, 0, k, 0)),
                  pl.BlockSpec((1, 1, FFT_N1), lambda k: (k, 0, 0)),
                  pl.BlockSpec((1, 1, FFT_N1), lambda k: (k, 0, 0)),
                  pl.BlockSpec((FFT_N1, FFT_N1), lambda k: (0, 0)),
                  pl.BlockSpec((FFT_N1, FFT_N1), lambda k: (0, 0))],
        out_specs=pl.BlockSpec((1, 2, FFT_N1, nparts // 2 * c), lambda k: (k, 0, 0, 0)),
        out_shape=jax.ShapeDtypeStruct((k2p, 2, FFT_N1, nparts // 2 * c), F32),
        compiler_params=_cparams(1),
        name="fft_stage_b_filter",
    )(ap, tabs["twr"], tabs["twi"], tabs["cr"], tabs["ci"])


def _fftb_conv_kernel(a_ref, g_ref, twr_ref, twi_ref, cr_ref, ci_ref, o_ref):
    mr, mi = _stage_b_matrix(twr_ref, twi_ref, cr_ref, ci_ref)
    fb = jnp.concatenate([jnp.concatenate([mr, -mi], axis=1),
                          jnp.concatenate([mi, mr], axis=1)], axis=0).astype(BF16)
    mrt, mit = mr.T, mi.T
    fbt = jnp.concatenate([jnp.concatenate([mrt, mit], axis=1),
                           jnp.concatenate([-mit, mrt], axis=1)], axis=0).astype(BF16)
    gr = g_ref[0, 0]
    gi = g_ref[0, 1]
    ngrp, nct = a_ref.shape[1], a_ref.shape[2]
    for b in range(a_ref.shape[0]):
        ar, ai = _unpack_pair(_load_stage_b(a_ref, b))
        a = jnp.concatenate([ar, ai], axis=0).astype(BF16)
        xh = jnp.dot(fb, a, preferred_element_type=F32)
        xr, xi = xh[:FFT_N1], xh[FFT_N1:]
        y = jnp.concatenate([xr * gr - xi * gi, xr * gi + xi * gr], axis=0).astype(BF16)
        cc = jnp.dot(fbt, y, preferred_element_type=F32)
        w = _pack_pair(cc[:FFT_N1], cc[FFT_N1:])
        for g in range(ngrp):
            for ct in range(nct):
                o_ref[b, g, ct] = w[g * SUBLANES:(g + 1) * SUBLANES, ct * LANES:(ct + 1) * LANES]


def _fft_stage_b_conv(ap, g, order, tabs):
    bsz, ngrp, nct = ap.shape[:3]
    k2p = tabs["k2p"]
    c = nct * LANES
    blk = pl.BlockSpec((bsz, ngrp, nct, SUBLANES, LANES), lambda k: (0, 0, 0, k, 0))
    return pl.pallas_call(
        _fftb_conv_kernel,
        grid=(k2p,),
        in_specs=[blk,
                  pl.BlockSpec((1, 2, FFT_N1, c), lambda k: (k, 0, 0, order)),
                  pl.BlockSpec((1, 1, FFT_N1), lambda k: (k, 0, 0)),
                  pl.BlockSpec((1, 1, FFT_N1), lambda k: (k, 0, 0)),
                  pl.BlockSpec((FFT_N1, FFT_N1), lambda k: (0, 0)),
                  pl.BlockSpec((FFT_N1, FFT_N1), lambda k: (0, 0))],
        out_specs=blk,
        out_shape=jax.ShapeDtypeStruct(ap.shape, U32),
        compiler_params=_cparams(1),
        name="fft_stage_b_conv",
    )(ap, g, tabs["twr"], tabs["twi"], tabs["cr"], tabs["ci"])


def _fftc_kernel(c_ref, gc_ref, u_ref, gate_ref, d_ref, o_ref, *, k2p, c):
    for j in range(SUBLANES):
        w = jnp.concatenate([c_ref[0, 0, ct, pl.ds(j, k2p, stride=SUBLANES), :] for ct in range(c // LANES)],
                            axis=1)
        re, im = _unpack_pair(w)
        cm = jnp.concatenate([re, im], axis=0).astype(BF16)
        y = jnp.dot(gc_ref[...], cm, preferred_element_type=F32)
        sl = slice(j * c, (j + 1) * c)
        o_ref[0, :, sl] = gate_ref[0, :, sl] * (y + d_ref[...] * u_ref[0, :, sl])


def _fft_stage_c(cp, gc, uf, gatef, d_row):
    bsz, ngrp, nct = cp.shape[:3]
    n2h = uf.shape[1]
    c = nct * LANES
    k2p = gc.shape[1] // 2
    fblk = pl.BlockSpec((1, n2h, SUBLANES * c), lambda b, g: (b, 0, g))
    return pl.pallas_call(
        functools.partial(_fftc_kernel, k2p=k2p, c=c),
        grid=(bsz, ngrp),
        in_specs=[pl.BlockSpec((1, 1, nct, k2p * SUBLANES, LANES), lambda b, g: (b, g, 0, 0, 0)),
                  pl.BlockSpec((n2h, 2 * k2p), lambda b, g: (0, 0)),
                  fblk, fblk,
                  pl.BlockSpec((1, c), lambda b, g: (0, 0))],
        out_specs=fblk,
        out_shape=jax.ShapeDtypeStruct(uf.shape, F32),
        compiler_params=_cparams(2),
        name="fft_stage_c",
    )(cp, gc, uf, gatef, d_row)


def _hyena_mix(hv, hx1, hx2, filt, skip, seq):
    c = skip.shape[1]
    tabs = _fft_tables(seq)
    g = _fft_stage_b_filter(_fft_stage_a(filt, tabs["fa"], c), tabs, 1.0 / tabs["n"])

    def conv(u, gate, order):
        cp = _fft_stage_b_conv(_fft_stage_a(u, tabs["fa"], c), g, order, tabs)
        return _fft_stage_c(cp, tabs["gc"], u, gate, skip[order].reshape(1, c))

    return conv(conv(hv, hx1, 0), hx2, 1)


def _qkprep_kernel(q_ref, k_ref, v_ref, qn_ref, kn_ref, oqt_ref, ok_ref, ovt_ref, *, tt, seq):
    i = pl.program_id(0)
    t = (i * tt + lax.broadcasted_iota(I32, (tt, HEAD_DIM), 0)) % seq
    lane = lax.broadcasted_iota(I32, (tt, HEAD_DIM), 1)
    j = lane % (HEAD_DIM // 2)
    quarter = HEAD_DIM // 4
    f = (j % quarter).astype(F32)
    inv = jnp.exp(f * (-2.0 * math.log(ROPE_THETA) / (HEAD_DIM // 2)))
    pos = jnp.where(j < quarter, t // GRID_W, t % GRID_W).astype(F32)
    ang = pos * inv
    cosv = jnp.cos(ang)
    sinv = jnp.sin(ang)
    sin_signed = jnp.where(lane < HEAD_DIM // 2, -sinv, sinv)

    def norm_rope(x, g, scale):
        ms = jnp.mean(x * x, axis=-1, keepdims=True)
        xn = x * lax.rsqrt(ms + NORM_EPS) * g
        return (xn * cosv + pltpu.roll(xn, HEAD_DIM // 2, axis=1) * sin_signed) * scale

    qg = qn_ref[...]
    kg = kn_ref[...]
    q_scale = LOG2_E * HEAD_DIM ** -0.5
    for h in range(N_HEADS):
        sl = slice(h * HEAD_DIM, (h + 1) * HEAD_DIM)
        oqt_ref[h] = norm_rope(q_ref[:, sl], qg, q_scale).T.astype(BF16)
    for h in range(N_KV_HEADS):
        sl = slice(h * HEAD_DIM, (h + 1) * HEAD_DIM)
        ok_ref[:, sl] = norm_rope(k_ref[:, sl], kg, 1.0).astype(BF16)
        ovt_ref[h, :HEAD_DIM, :] = v_ref[:, sl].T.astype(BF16)
        extra = lax.broadcasted_iota(I32, (V_ROWS - HEAD_DIM, tt), 0)
        ovt_ref[h, HEAD_DIM:, :] = jnp.where(extra == 0, 1.0, 0.0).astype(BF16)


def _qk_prep(proj, qn, kn, seq):
    n = proj.shape[0]
    tt = min(512, seq)
    qw = N_HEADS * HEAD_DIM
    kw = N_KV_HEADS * HEAD_DIM
    q_off = (3 * HY_WIDTH) // qw
    k_off = (3 * HY_WIDTH + qw) // kw
    return pl.pallas_call(
        functools.partial(_qkprep_kernel, tt=tt, seq=seq),
        grid=(n // tt,),
        in_specs=[pl.BlockSpec((tt, qw), lambda i: (i, q_off)),
                  pl.BlockSpec((tt, kw), lambda i: (i, k_off)),
                  pl.BlockSpec((tt, kw), lambda i: (i, k_off + 1)),
                  pl.BlockSpec((1, HEAD_DIM), lambda i: (0, 0)),
                  pl.BlockSpec((1, HEAD_DIM), lambda i: (0, 0))],
        out_specs=[pl.BlockSpec((N_HEADS, HEAD_DIM, tt), lambda i: (0, 0, i)),
                   pl.BlockSpec((tt, kw), lambda i: (i, 0)),
                   pl.BlockSpec((N_KV_HEADS, V_ROWS, tt), lambda i: (0, 0, i))],
        out_shape=[jax.ShapeDtypeStruct((N_HEADS, HEAD_DIM, n), BF16),
                   jax.ShapeDtypeStruct((n, kw), BF16),
                   jax.ShapeDtypeStruct((N_KV_HEADS, V_ROWS, n), BF16)],
        compiler_params=_cparams(1),
        name="qk_prep",
    )(proj, proj, proj, qn, kn)


def _flash_kernel(qt_ref, k_ref, vt_ref, o_ref, acc_sc, s_sc, mc_sc, *, tq, tk, seq, unroll):
    nq = Q_PER_KV * tq
    nk = seq // tk
    qt = jnp.concatenate([qt_ref[g] for g in range(Q_PER_KV)], axis=1)
    acc_sc[...] = jnp.zeros(acc_sc.shape, F32)

    def scores(c, slot):
        start = pl.multiple_of(c * tk, tk)
        s = jnp.dot(k_ref[pl.ds(start, tk), :], qt, preferred_element_type=F32)
        s_sc[slot] = s
        mc_sc[slot] = jnp.max(s, axis=0, keepdims=True)

    scores(0, 0)

    def trip(i, m, last):
        for u in range(unroll):
            c = i * unroll + u
            if not (last and u == unroll - 1):
                scores(c + 1, (u + 1) % 2)
            s = s_sc[u % 2]
            mc = mc_sc[u % 2]
            start = pl.multiple_of(c * tk, tk)
            vc = vt_ref[0, :, pl.ds(start, tk)]
            p = jnp.exp2(s - mc).astype(BF16)
            oc = jnp.dot(vc, p, preferred_element_type=F32)
            m_new = jnp.maximum(m, mc)
            acc_sc[...] = acc_sc[...] * jnp.exp2(m - m_new) + oc * jnp.exp2(mc - m_new)
            m = m_new
        return m

    ntrips = nk // unroll
    m = lax.fori_loop(0, ntrips - 1, lambda i, m: trip(i, m, False), jnp.full((1, nq), -jnp.inf, F32))
    trip(ntrips - 1, m, True)
    o = acc_sc[:HEAD_DIM, :] / acc_sc[HEAD_DIM:HEAD_DIM + 1, :]
    for g in range(Q_PER_KV):
        o_ref[:, g * HEAD_DIM:(g + 1) * HEAD_DIM] = o[:, g * tq:(g + 1) * tq].T


def _flash_attention(qt, k, vt, bsz, seq):
    n = bsz * seq
    tq = min(256, seq)
    tk = min(1024, seq // 2)
    unroll = 4 if seq // tk >= 8 else (2 if seq // tk >= 2 else 1)
    gw = Q_PER_KV * HEAD_DIM
    nqb = seq // tq
    return pl.pallas_call(
        functools.partial(_flash_kernel, tq=tq, tk=tk, seq=seq, unroll=unroll),
        grid=(bsz, N_KV_HEADS, nqb),
        in_specs=[pl.BlockSpec((Q_PER_KV, HEAD_DIM, tq), lambda b, h, i: (h, 0, b * nqb + i)),
                  pl.BlockSpec((seq, HEAD_DIM), lambda b, h, i: (b, h)),
                  pl.BlockSpec((1, V_ROWS, seq), lambda b, h, i: (h, 0, b))],
        out_specs=pl.BlockSpec((tq, gw), lambda b, h, i: (b * nqb + i, h)),
        out_shape=jax.ShapeDtypeStruct((n, N_HEADS * HEAD_DIM), F32),
        scratch_shapes=[pltpu.VMEM((V_ROWS, Q_PER_KV * tq), F32),
                        pltpu.VMEM((2, tk, Q_PER_KV * tq), F32),
                        pltpu.VMEM((2, 1, Q_PER_KV * tq), F32)],
        compiler_params=_cparams(3),
        name="flash_attention",
    )(qt, k, vt)


def _layer_norm(y, g, b):
    mu = jnp.mean(y, axis=-1, keepdims=True)
    yc = y - mu
    var = jnp.mean(yc * yc, axis=-1, keepdims=True)
    return yc * lax.rsqrt(var + NORM_EPS) * g + b


def _merge_kernel(yh_ref, ya_ref, x_ref, gh_ref, ga_ref, wo_ref, lg_ref, lb_ref, wr_ref, br_ref,
                  x1_ref, aff_ref):
    hw = yh_ref.shape[1]
    yh = yh_ref[...]
    ya = ya_ref[...]
    mh = yh * lax.rsqrt(jnp.mean(yh * yh, axis=-1, keepdims=True) + NORM_EPS) * gh_ref[...]
    ma = ya * lax.rsqrt(jnp.mean(ya * ya, axis=-1, keepdims=True) + NORM_EPS) * ga_ref[...]
    y = (jnp.dot(mh.astype(BF16), wo_ref[:hw, :], preferred_element_type=F32)
         + jnp.dot(ma.astype(BF16), wo_ref[hw:, :], preferred_element_type=F32)
         + DN_ALPHA * x_ref[...])
    x1 = _layer_norm(y, lg_ref[...], lb_ref[...])
    x1_ref[...] = x1
    x_hi = x1.astype(BF16)
    x_lo = (x1 - x_hi.astype(F32)).astype(BF16)
    both = jnp.dot(x_hi, wr_ref[...], preferred_element_type=F32)
    logits = (both[:, :LANES] + both[:, LANES:]
              + jnp.dot(x_lo, wr_ref[:, :LANES], preferred_element_type=F32) + br_ref[...])
    e = jnp.exp(logits - jnp.max(logits, axis=-1, keepdims=True))
    aff_ref[...] = e / jnp.sum(e, axis=-1, keepdims=True)


def _merge(y_hy, y_att, x2, g_hy, g_attn, wo_bf, ln_g, ln_b, wr_pad, br_pad):
    n, d = x2.shape
    hw = y_hy.shape[1]
    tm = min(256, n)
    row = lambda w: pl.BlockSpec((1, w), lambda i: (0, 0))
    return pl.pallas_call(
        _merge_kernel,
        grid=(n // tm,),
        in_specs=[pl.BlockSpec((tm, hw), lambda i: (i, 0)),
                  pl.BlockSpec((tm, d - hw), lambda i: (i, 0)),
                  pl.BlockSpec((tm, d), lambda i: (i, 0)),
                  row(hw), row(d - hw),
                  pl.BlockSpec((d, d), lambda i: (0, 0)),
                  row(d), row(d),
                  pl.BlockSpec((d, 2 * LANES), lambda i: (0, 0)),
                  row(LANES)],
        out_specs=[pl.BlockSpec((tm, d), lambda i: (i, 0)),
                   pl.BlockSpec((tm, LANES), lambda i: (i, 0))],
        out_shape=[jax.ShapeDtypeStruct((n, d), F32), jax.ShapeDtypeStruct((n, LANES), F32)],
        compiler_params=_cparams(1),
        name="merge_ln_router",
    )(y_hy, y_att, x2, g_hy, g_attn, wo_bf, ln_g, ln_b, wr_pad, br_pad)


def _select_kernel(a_ref, idx_ref, gslot_ref, pos_ref, excl_ref, *, cap):
    a = a_ref[0]
    ne, nr, _ = a.shape
    bits = lax.bitcast_convert_type(a, I32)

    def count(mask):
        return jnp.sum(jnp.sum(mask.astype(F32), axis=2, keepdims=True), axis=1, keepdims=True)

    thr = jnp.zeros((ne, 1, 1), I32)
    for bit in range(30, -1, -1):
        cand = thr | (1 << bit)
        thr = jnp.where(count(bits >= cand) >= cap, cand, thr)

    ii = lax.broadcasted_iota(I32, (LANES, LANES), 0)
    jj = lax.broadcasted_iota(I32, (LANES, LANES), 1)
    upper = (ii <= jj).astype(BF16)
    ri = lax.broadcasted_iota(I32, (ne, nr, nr), 1)
    rj = lax.broadcasted_iota(I32, (ne, nr, nr), 2)
    lower_strict = (rj < ri).astype(BF16)

    def prefix(mask):
        mb = mask.astype(BF16).reshape(ne * nr, LANES)
        pin = jnp.dot(mb, upper, preferred_element_type=F32).reshape(ne, nr, LANES)
        tot = pin[:, :, LANES - 1:LANES]
        totb = jnp.broadcast_to(tot, (ne, nr, LANES)).astype(BF16)
        off = jnp.einsum("ers,esl->erl", lower_strict, totb, preferred_element_type=F32)
        return pin, tot, off[:, :, 0:1]

    gt = bits > thr
    eq = bits == thr
    need = cap - count(gt)
    pin_eq, _, off_eq = prefix(eq)
    eq_rank = pin_eq + off_eq - eq.astype(F32)
    chosen = gt | (eq & (eq_rank < need))
    pin, tot, off = prefix(chosen)
    chosen_f = chosen.astype(F32)
    excl = pin + off - chosen_f
    excl_ref[0] = excl
    pos_ref[0] = jnp.where(chosen, excl, -1.0)

    s_row = lax.broadcasted_iota(I32, (nr, cap), 1).astype(F32)
    r_col = lax.broadcasted_iota(I32, (nr, cap), 0).astype(F32)
    j_col = lax.broadcasted_iota(I32, (LANES, cap), 0).astype(F32)
    for e in range(ne):
        lo = off[e]
        hi = lo + tot[e]
        in_row = ((lo <= s_row) & (s_row < hi)).astype(F32)
        in_row_b = in_row.astype(BF16)
        row_of_s = jnp.sum(in_row * r_col, axis=0, keepdims=True)
        s_local = s_row[0:1] - jnp.sum(in_row * lo, axis=0, keepdims=True)
        pin_t = pin[e].T.astype(BF16)
        g = jnp.dot(pin_t, in_row_b, preferred_element_type=F32)
        lane_of_s = jnp.sum((g <= s_local).astype(F32), axis=0, keepdims=True)
        idx_ref[0, e:e + 1, :] = (row_of_s * LANES + lane_of_s).astype(I32)
        rem = a[e]
        a_of_s = jnp.zeros((LANES, cap), F32)
        for _ in range(3):
            part = rem.astype(BF16)
            rem = rem - part.astype(F32)
            a_of_s = a_of_s + jnp.dot(part.astype(F32).T.astype(BF16), in_row_b, preferred_element_type=F32)
        gslot_ref[0, e:e + 1, :] = jnp.sum(jnp.where(j_col == lane_of_s, a_of_s, 0.0), axis=0, keepdims=True)


def _select(aff4, cap):
    bsz, ne, nr, _ = aff4.shape
    blk = pl.BlockSpec((1, ne, nr, LANES), lambda b: (b, 0, 0, 0))
    slot = pl.BlockSpec((1, ne, cap), lambda b: (b, 0, 0))
    return pl.pallas_call(
        functools.partial(_select_kernel, cap=cap),
        grid=(bsz,),
        in_specs=[blk],
        out_specs=[slot, slot, blk, blk],
        out_shape=[jax.ShapeDtypeStruct((bsz, ne, cap), I32),
                   jax.ShapeDtypeStruct((bsz, ne, cap), F32),
                   jax.ShapeDtypeStruct(aff4.shape, F32),
                   jax.ShapeDtypeStruct(aff4.shape, F32)],
        compiler_params=_cparams(1),
        name="expert_choice_select",
    )(aff4)


def _expert_kernel(idx_ref, idxn_ref, x_hbm, g_ref, wg_ref, wu_ref, wd_ref, o_ref, xg_a, xg_b, sem,
                   *, tc, nchunk, seq, ff, npair, bsz, nsteps):
    s = pl.program_id(0)

    def issue(ids_ref, first, step, buf, k):
        base = ((step // npair) % bsz) * seq
        for r in range(tc):
            pltpu.make_async_copy(x_hbm.at[pl.ds(base + ids_ref[0, 0, first + r], 1)],
                                  buf.at[pl.ds(r, 1)], sem.at[k]).start()

    def wait_rows(buf, k):
        pltpu.make_async_copy(x_hbm.at[pl.ds(0, tc)], buf, sem.at[k]).wait()

    def ffn(buf, first):
        xb = buf[...].astype(BF16)
        gate = jnp.dot(xb, wg_ref[0], preferred_element_type=F32)
        h = gate * jax.nn.sigmoid(gate) * jnp.dot(xb, wu_ref[0], preferred_element_type=F32)
        y = jnp.dot(h.astype(BF16), wd_ref[0], preferred_element_type=F32)
        o_ref[0, 0, first:first + tc, :] = (y * g_ref[0, first:first + tc, :]).astype(BF16)

    @pl.when(s == 0)
    def _():
        issue(idx_ref, 0, s, xg_a, 0)

    bufs = (xg_a, xg_b)
    for k in range(nchunk):
        wait_rows(bufs[k % 2], k % 2)
        if k + 1 < nchunk:
            issue(idx_ref, (k + 1) * tc, s, bufs[(k + 1) % 2], (k + 1) % 2)
        else:
            issue(idxn_ref, 0, jnp.minimum(s + 1, nsteps - 1), xg_a, 0)
        ffn(bufs[k % 2], k * tc)

    @pl.when(s == nsteps - 1)
    def _():
        wait_rows(xg_a, 0)


def _experts(idx, gslot, x1, wg_bf, wu_bf, wd_bf, bsz, seq):
    _, ne, cap = idx.shape
    d = x1.shape[1]
    ff = wd_bf.shape[1]
    tc = min(256, cap // 2)
    nchunk = 4 if cap % (4 * tc) == 0 else 2
    rows = nchunk * tc
    npair = cap // rows
    nsteps = ne * bsz * npair
    idx_s = idx.transpose(1, 0, 2).reshape(nsteps, 1, rows)
    g_s = gslot.transpose(1, 0, 2).reshape(nsteps, rows, 1)
    out_map = lambda s: ((s // npair) % bsz, s // (bsz * npair), s % npair, 0)
    wmap = lambda s: (s // (bsz * npair), 0, 0)
    return pl.pallas_call(
        functools.partial(_expert_kernel, tc=tc, nchunk=nchunk, seq=seq, ff=ff, npair=npair, bsz=bsz,
                          nsteps=nsteps),
        grid=(nsteps,),
        in_specs=[pl.BlockSpec((1, 1, rows), lambda s: (s, 0, 0), memory_space=pltpu.SMEM),
                  pl.BlockSpec((1, 1, rows), lambda s: (jnp.minimum(s + 1, nsteps - 1), 0, 0),
                               memory_space=pltpu.SMEM),
                  pl.BlockSpec(memory_space=pl.ANY),
                  pl.BlockSpec((1, rows, 1), lambda s: (s, 0, 0)),
                  pl.BlockSpec((1, d, ff), wmap),
                  pl.BlockSpec((1, d, ff), wmap),
                  pl.BlockSpec((1, ff, d), wmap)],
        out_specs=pl.BlockSpec((1, 1, rows, d), out_map),
        out_shape=jax.ShapeDtypeStruct((bsz, ne, cap, d), BF16),
        scratch_shapes=[pltpu.VMEM((tc, d), F32), pltpu.VMEM((tc, d), F32), pltpu.SemaphoreType.DMA((2,))],
        compiler_params=_cparams(1),
        name="expert_ffn",
    )(idx_s, idx_s, x1, g_s, wg_bf, wu_bf, wd_bf)


def _combine_kernel(w0_ref, y_hbm, x1_ref, pos_ref, lg_ref, lb_ref, o_ref, ybuf, xbuf, acc_ref, sem, xsem,
                    *, tb_tokens, win, cap, ne, ntb):
    s = pl.program_id(0)
    nsteps = pl.num_programs(0)
    slot = s % 2

    def window_copies(step, to_slot):
        bb = step // ntb
        wbase = (bb * (ntb + 1) + step % ntb) * ne
        copies, firsts = [], []
        for e in range(ne):
            sa = jnp.minimum((w0_ref[wbase + e] // BF16_ROWS) * BF16_ROWS, cap - win)
            sa = pl.multiple_of(sa, BF16_ROWS)
            firsts.append(sa)
            copies.append(pltpu.make_async_copy(y_hbm.at[bb, e, pl.ds(sa, win)],
                                                ybuf.at[to_slot, pl.ds(e * win, win)], sem.at[to_slot, e]))
        return copies, firsts

    @pl.when(s == 0)
    def _():
        for cp in window_copies(s, 0)[0]:
            cp.start()

    @pl.when(s + 1 < nsteps)
    def _():
        for cp in window_copies(s + 1, 1 - slot)[0]:
            cp.start()

    b = s // ntb
    base = (b * (ntb + 1) + s % ntb) * ne
    copies, starts = window_copies(s, slot)
    lane = lax.broadcasted_iota(I32, (tb_tokens, win), 1).astype(F32)
    onehots = [(pos_ref[0, :, e:e + 1] - starts[e].astype(F32) == lane).astype(BF16) for e in range(ne)]
    p = jnp.concatenate(onehots, axis=1)
    for cp in copies:
        cp.wait()
    acc_ref[...] = DN_ALPHA * x1_ref[...] + jnp.dot(p, ybuf[slot], preferred_element_type=F32)

    for e in range(ne):
        first_end = starts[e] + win
        n_extra = jnp.maximum(w0_ref[base + ne + e] - first_end + win - 1, 0) // win

        def extra(w, carry, e=e, first_end=first_end):
            nominal = first_end + w * win
            st = pl.multiple_of(jnp.minimum(nominal, cap - win), BF16_ROWS)
            cp = pltpu.make_async_copy(y_hbm.at[b, e, pl.ds(st, win)], xbuf, xsem)
            cp.start()
            cp.wait()
            pe = pos_ref[0, :, e:e + 1]
            oh = ((pe - st.astype(F32) == lane) & (pe >= nominal.astype(F32))).astype(BF16)
            acc_ref[...] += jnp.dot(oh, xbuf[...], preferred_element_type=F32)
            return carry

        lax.fori_loop(0, n_extra, extra, 0)
    o_ref[...] = _layer_norm(acc_ref[...], lg_ref[...], lb_ref[...])


def _combine(y, x1, pos_t, w0, ln_g, ln_b, bsz, seq, tbt):
    _, ne, cap, d = y.shape
    win = min(LANES, cap)
    ntb = seq // tbt
    grid_spec = pltpu.PrefetchScalarGridSpec(
        num_scalar_prefetch=1,
        grid=(bsz * ntb,),
        in_specs=[pl.BlockSpec(memory_space=pl.ANY),
                  pl.BlockSpec((tbt, d), lambda s, w: (s, 0)),
                  pl.BlockSpec((1, tbt, ne), lambda s, w: (s // ntb, s % ntb, 0)),
                  pl.BlockSpec((1, d), lambda s, w: (0, 0)),
                  pl.BlockSpec((1, d), lambda s, w: (0, 0))],
        out_specs=pl.BlockSpec((tbt, d), lambda s, w: (s, 0)),
        scratch_shapes=[pltpu.VMEM((2, ne * win, d), BF16), pltpu.VMEM((win, d), BF16),
                        pltpu.VMEM((tbt, d), F32),
                        pltpu.SemaphoreType.DMA((2, ne)), pltpu.SemaphoreType.DMA(())],
    )
    return pl.pallas_call(
        functools.partial(_combine_kernel, tb_tokens=tbt, win=win, cap=cap, ne=ne, ntb=ntb),
        grid_spec=grid_spec,
        out_shape=jax.ShapeDtypeStruct((bsz * seq, d), F32),
        compiler_params=_cparams(1),
        name="moe_combine_ln",
    )(w0, y, x1, pos_t, ln_g, ln_b)


def _deinterleave(width):
    idx = np.arange(width).reshape(-1, HEAD_DIM // 2, 2)
    return np.concatenate([idx[:, :, 0], idx[:, :, 1]], axis=1).reshape(-1)


def kernel(x, w_in, b_in, hy_conv_w, hy_conv_b, hy_ffn_w1, hy_ffn_b1, hy_sin_f1, hy_ffn_w2, hy_ffn_b2, hy_sin_f2, hy_ffn_w3, hy_decay, hy_skip, q_norm, k_norm, g_hy, g_attn, w_out, ln1_g, ln1_b, w_router, b_router, w_gate, w_up, w_down, ln2_g, ln2_b):
    bsz, seq, d = x.shape
    n = bsz * seq
    x2 = x.reshape(n, d)
    s1 = 3 * HY_WIDTH
    qw = N_HEADS * HEAD_DIM
    kw = N_KV_HEADS * HEAD_DIM
    perm = np.concatenate([np.arange(s1), s1 + _deinterleave(qw), s1 + qw + _deinterleave(kw),
                           np.arange(s1 + qw + kw, w_in.shape[1])])
    w_in_bf = w_in[:, perm].astype(BF16)
    b_in_p = b_in[perm].reshape(1, -1)
    hd_perm = _deinterleave(HEAD_DIM)

    proj = _in_projection(x2, w_in_bf, b_in_p)

    conv_b = hy_conv_b.reshape(1, -1)
    hv, hx1, hx2 = [_short_conv(proj, hy_conv_w, conv_b, part, bsz, seq) for part in range(3)]
    filt = _hyena_filters(seq, hy_ffn_w1, hy_ffn_b1, hy_sin_f1, hy_ffn_w2, hy_ffn_b2, hy_sin_f2,
                          hy_ffn_w3, hy_decay)
    y_hy = _hyena_mix(hv, hx1, hx2, filt, hy_skip, seq).reshape(n, HY_WIDTH)

    qt, kb, vt = _qk_prep(proj, q_norm[hd_perm].reshape(1, HEAD_DIM), k_norm[hd_perm].reshape(1, HEAD_DIM), seq)
    y_att = _flash_attention(qt, kb, vt, bsz, seq)

    ne = w_router.shape[1]
    wr_hi = w_router.astype(BF16)
    wr_lo = (w_router - wr_hi.astype(F32)).astype(BF16)
    wr_pair = (jnp.zeros((d, 2 * LANES), BF16).at[:, :ne].set(wr_hi).at[:, LANES:LANES + ne].set(wr_lo))
    br_pad = jnp.full((1, LANES), -1e30, F32).at[0, :ne].set(b_router)
    x1, aff = _merge(y_hy, y_att, x2, g_hy.reshape(1, -1), g_attn.reshape(1, -1),
                     w_out.astype(BF16), ln1_g.reshape(1, d), ln1_b.reshape(1, d), wr_pair, br_pad)

    cap = CAPACITY_FACTOR * seq // ne
    aff4 = aff[:, :ne].reshape(bsz, seq, ne).transpose(0, 2, 1).reshape(bsz, ne, seq // LANES, LANES)
    idx, gslot, pos, excl = _select(aff4, cap)
    y = _experts(idx, gslot, x1, w_gate.astype(BF16), w_up.astype(BF16), w_down.astype(BF16), bsz, seq)
    pos_t = pos.reshape(bsz, ne, seq).transpose(0, 2, 1)
    tbt = min(256, cap)
    before = excl.reshape(bsz, ne, seq // tbt, tbt)[:, :, :, 0].transpose(0, 2, 1)
    w0 = jnp.concatenate([before, jnp.full((bsz, 1, ne), cap, F32)], axis=1).reshape(-1).astype(I32)
    out = _combine(y, x1, pos_t, w0, ln2_g.reshape(1, d), ln2_b.reshape(1, d), bsz, seq, tbt)
    return out.reshape(bsz, seq, d)
```

```python
import functools
import math

import numpy as np
import jax
import jax.numpy as jnp
from jax import lax
from jax.experimental import pallas as pl
from jax.experimental.pallas import tpu as pltpu

F32 = jnp.float32
BF16 = jnp.bfloat16
I32 = jnp.int32
U32 = jnp.uint32
HIGHEST = lax.Precision.HIGHEST

HY_WIDTH = 1024
HY_BANDS = 16
HEAD_DIM = 128
N_HEADS = 8
N_KV_HEADS = 2
Q_PER_KV = N_HEADS // N_KV_HEADS
ROPE_THETA = 10000.0
GRID_W = 64
N_EXPERTS = 16
CAPACITY_FACTOR = 2
NORM_EPS = 1e-6
DN_ALPHA = 2.0 ** 0.25
LOG2_E = math.log2(math.e)
V_ROWS = HEAD_DIM + 16

LANES = 128
SUBLANES = 8
BF16_ROWS = 16
FFT_N1 = 128
VMEM_LIMIT = 56 * 1024 * 1024


def _cparams(n_axes, vmem=VMEM_LIMIT):
    return pltpu.CompilerParams(dimension_semantics=("arbitrary",) * n_axes, vmem_limit_bytes=vmem)


def _round_up(a, m):
    return (a + m - 1) // m * m


def _inproj_kernel(x_ref, w_ref, b_ref, o_ref, xb_ref):
    @pl.when(pl.program_id(1) == 0)
    def _():
        xb_ref[...] = x_ref[...].astype(BF16)

    o_ref[...] = jnp.dot(xb_ref[...], w_ref[...], preferred_element_type=F32) + b_ref[...]


def _in_projection(x2, w_bf, b_row):
    n, d = x2.shape
    wdt = w_bf.shape[1]
    tm = min(1024, n)
    tn = wdt // 3
    return pl.pallas_call(
        _inproj_kernel,
        grid=(n // tm, wdt // tn),
        in_specs=[pl.BlockSpec((tm, d), lambda i, j: (i, 0)),
                  pl.BlockSpec((d, tn), lambda i, j: (0, j)),
                  pl.BlockSpec((1, tn), lambda i, j: (0, j))],
        out_specs=pl.BlockSpec((tm, tn), lambda i, j: (i, j)),
        out_shape=jax.ShapeDtypeStruct((n, wdt), F32),
        scratch_shapes=[pltpu.VMEM((tm, d), BF16)],
        compiler_params=_cparams(2),
        name="in_projection",
    )(x2, w_bf, b_row)


def _store_fft_layout(res, o_ref, scr):
    tt, c = res.shape
    nct = c // LANES
    for ct in range(nct):
        scr[ct] = res[:, ct * LANES:(ct + 1) * LANES]

    for n1 in range(FFT_N1):
        for ct in range(nct):
            col = n1 * c + ct * LANES
            o_ref[0, :, col:col + LANES] = scr[ct, pl.ds(n1, tt // FFT_N1, stride=FFT_N1), :]


def _shortconv_kernel(m_ref, p_ref, nx_ref, w_ref, b_ref, o_ref, scr, *, tt, seq):
    i = pl.program_id(0)
    r_in = lax.broadcasted_iota(I32, (tt, 1), 0)
    tpos = (i * tt + r_in) % seq
    cur = m_ref[...]
    up = pltpu.roll(cur, 1, axis=0)
    up = jnp.where(r_in == 0, p_ref[SUBLANES - 1:SUBLANES, :], up)
    up = jnp.where(tpos == 0, 0.0, up)
    dn = pltpu.roll(cur, tt - 1, axis=0)
    dn = jnp.where(r_in == tt - 1, nx_ref[0:1, :], dn)
    dn = jnp.where(tpos == seq - 1, 0.0, dn)
    wv = w_ref[...]
    _store_fft_layout(wv[0:1] * up + wv[1:2] * cur + wv[2:3] * dn + b_ref[...], o_ref, scr)


def _short_conv(proj, conv_w, conv_b, part, bsz, seq):
    n = proj.shape[0]
    c = HY_WIDTH
    tt = SUBLANES * FFT_N1
    nrb = n // SUBLANES
    hb = tt // SUBLANES
    tps = seq // tt
    return pl.pallas_call(
        functools.partial(_shortconv_kernel, tt=tt, seq=seq),
        grid=(n // tt,),
        in_specs=[pl.BlockSpec((tt, c), lambda i: (i, part)),
                  pl.BlockSpec((SUBLANES, c), lambda i: (jnp.maximum(i * hb - 1, 0), part)),
                  pl.BlockSpec((SUBLANES, c), lambda i: (jnp.minimum((i + 1) * hb, nrb - 1), part)),
                  pl.BlockSpec((3, c), lambda i: (0, part)),
                  pl.BlockSpec((1, c), lambda i: (0, part))],
        out_specs=pl.BlockSpec((1, SUBLANES, FFT_N1 * c), lambda i: (i // tps, i % tps, 0)),
        out_shape=jax.ShapeDtypeStruct((bsz, seq // FFT_N1, FFT_N1 * c), F32),
        scratch_shapes=[pltpu.VMEM((c // LANES, tt, LANES), F32)],
        compiler_params=_cparams(1),
        name="short_conv",
    )(proj, proj, proj, conv_w, conv_b)


def _filter_kernel(freq_ref, w1_ref, b1_ref, f1_ref, w2_ref, b2_ref, f2_ref, w3_ref, dec_ref, o_ref, h_sc,
                   *, tt, seq, nparts):
    i = pl.program_id(0)
    part = pl.program_id(1)

    def time_of_row(shape):
        rr = lax.broadcasted_iota(I32, shape, 0)
        return i * tt + rr // SUBLANES + FFT_N1 * (rr % SUBLANES)

    @pl.when(part == 0)
    def _():
        r = time_of_row((tt, LANES)).astype(F32)
        lane = lax.broadcasted_iota(I32, (tt, LANES), 1)
        t = r / float(seq - 1)
        w = (2.0 * math.pi / seq) * r
        arg = freq_ref[...] * w
        emb = jnp.where(lane == 0, t,
                        jnp.where(lane <= HY_BANDS, jnp.cos(arg),
                                  jnp.where(lane <= 2 * HY_BANDS, -jnp.sin(arg), 0.0)))
        h = jnp.sin(f1_ref[...] * (jnp.dot(emb, w1_ref[...], preferred_element_type=F32, precision=HIGHEST)
                                   + b1_ref[...]))
        h_sc[...] = jnp.sin(f2_ref[...] * (jnp.dot(h, w2_ref[...], preferred_element_type=F32,
                                                   precision=HIGHEST) + b2_ref[...]))

    hf = h_sc[...]
    wf = w3_ref[...]
    h_hi = hf.astype(BF16)
    h_lo = (hf - h_hi.astype(F32)).astype(BF16)
    w_hi = wf.astype(BF16)
    w_lo = (wf - w_hi.astype(F32)).astype(BF16)
    h3 = (jnp.dot(h_hi, w_hi, preferred_element_type=F32) + jnp.dot(h_lo, w_hi, preferred_element_type=F32)
          + jnp.dot(h_hi, w_lo, preferred_element_type=F32))
    row_t = time_of_row((tt, 1))
    win = jnp.exp(-(row_t.astype(F32) / float(seq - 1)) * jnp.abs(dec_ref[...]))
    out = h3 * win
    out = jnp.where((row_t == 0) & (part >= nparts // 2), 0.0, out)
    c = out.shape[1]
    for n1 in range(FFT_N1):
        o_ref[0, :, n1 * c:(n1 + 1) * c] = out[n1 * SUBLANES:(n1 + 1) * SUBLANES, :]


def _hyena_filters(seq, w1, b1, f1, w2, b2, f2, w3, decay):
    ffn = w1.shape[1]
    c = HY_WIDTH
    nparts = w3.shape[1] // c
    tt = SUBLANES * FFT_N1
    band = jnp.linspace(1e-4, HY_BANDS - 1, HY_BANDS, dtype=F32)
    freq = jnp.zeros((1, LANES), F32).at[0, 1:1 + HY_BANDS].set(band).at[0, 1 + HY_BANDS:1 + 2 * HY_BANDS].set(band)
    w1p = jnp.zeros((LANES, ffn), F32).at[:w1.shape[0]].set(w1)
    full = lambda shape: pl.BlockSpec(shape, lambda i, p: (0,) * len(shape))
    return pl.pallas_call(
        functools.partial(_filter_kernel, tt=tt, seq=seq, nparts=nparts),
        grid=(seq // tt, nparts),
        in_specs=[full((1, LANES)), full((LANES, ffn)), full((1, ffn)), full((1, ffn)),
                  full((ffn, ffn)), full((1, ffn)), full((1, ffn)),
                  pl.BlockSpec((ffn, c), lambda i, p: (0, p)), pl.BlockSpec((1, c), lambda i, p: (0, p))],
        out_specs=pl.BlockSpec((1, SUBLANES, FFT_N1 * c), lambda i, p: (p, i, 0)),
        out_shape=jax.ShapeDtypeStruct((nparts, seq // FFT_N1, FFT_N1 * c), F32),
        scratch_shapes=[pltpu.VMEM((tt, ffn), F32)],
        compiler_params=_cparams(2),
        name="hyena_filters",
    )(freq, w1p, b1.reshape(1, ffn), f1.reshape(1, ffn), w2, b2.reshape(1, ffn), f2.reshape(1, ffn),
      w3, decay.reshape(1, -1))


def _fft_tables(seq):
    n2h = seq // FFT_N1
    n2 = 2 * n2h
    n = FFT_N1 * n2
    k2n = n2h + 1
    k2p = _round_up(k2n, BF16_ROWS)
    kk = np.arange(k2p)[:, None]
    nn = np.arange(n2h)[None, :]
    ang = 2.0 * np.pi * ((kk * nn) % n2) / n2
    valid = (kk < k2n).astype(np.float64)
    fa = np.concatenate([np.cos(ang) * valid, -np.sin(ang) * valid], axis=0)
    wk = np.where((kk == 0) | (kk == n2h), 1.0, 2.0) * valid
    gc = np.concatenate([(np.cos(ang) * wk).T, (-np.sin(ang) * wk).T], axis=1)
    n1 = np.arange(FFT_N1)[None, :]
    tw_ang = 2.0 * np.pi * ((kk * n1) % n) / n
    twr = (np.cos(tw_ang) * valid).reshape(k2p, 1, FFT_N1)
    twi = (-np.sin(tw_ang) * valid).reshape(k2p, 1, FFT_N1)
    k1 = np.arange(FFT_N1)[:, None]
    base = 2.0 * np.pi * ((k1 * n1) % FFT_N1) / FFT_N1
    return dict(n2h=n2h, k2p=k2p, n=n,
                fa=jnp.asarray(fa, BF16), gc=jnp.asarray(gc, BF16),
                twr=jnp.asarray(twr, F32), twi=jnp.asarray(twi, F32),
                cr=jnp.asarray(np.cos(base), F32), ci=jnp.asarray(-np.sin(base), F32))


def _pack_pair(re, im):
    hi = lax.bitcast_convert_type(re.astype(BF16).astype(F32), U32)
    lo = lax.bitcast_convert_type(im.astype(BF16).astype(F32), U32)
    return hi | (lo >> 16)


def _unpack_pair(w):
    re = lax.bitcast_convert_type(w & jnp.uint32(0xFFFF0000), F32)
    im = lax.bitcast_convert_type(w << 16, F32)
    return re, im


def _ffta_kernel(x_ref, fa_ref, o_ref, *, k2p, c):
    for j in range(SUBLANES):
        xs = x_ref[0, :, j * c:(j + 1) * c].astype(BF16)
        res = jnp.dot(fa_ref[...], xs, preferred_element_type=F32)
        w = _pack_pair(res[:k2p], res[k2p:])
        for ct in range(c // LANES):
            o_ref[0, 0, ct, pl.ds(j, k2p, stride=SUBLANES), :] = w[:, ct * LANES:(ct + 1) * LANES]


def _fft_stage_a(xf, fa, c):
    bx, n2h, _ = xf.shape
    k2p = fa.shape[0] // 2
    ngrp = FFT_N1 // SUBLANES
    nct = c // LANES
    return pl.pallas_call(
        functools.partial(_ffta_kernel, k2p=k2p, c=c),
        grid=(bx, ngrp),
        in_specs=[pl.BlockSpec((1, n2h, SUBLANES * c), lambda b, g: (b, 0, g)),
                  pl.BlockSpec((2 * k2p, n2h), lambda b, g: (0, 0))],
        out_specs=pl.BlockSpec((1, 1, nct, k2p * SUBLANES, LANES), lambda b, g: (b, g, 0, 0, 0)),
        out_shape=jax.ShapeDtypeStruct((bx, ngrp, nct, k2p * SUBLANES, LANES), U32),
        compiler_params=_cparams(2),
        name="fft_stage_a",
    )(xf, fa)


def _load_stage_b(a_ref, b):
    ngrp, nct = a_ref.shape[1], a_ref.shape[2]
    return jnp.concatenate([jnp.concatenate([a_ref[b, g, ct] for ct in range(nct)], axis=1)
                            for g in range(ngrp)], axis=0)


def _stage_b_matrix(twr_ref, twi_ref, cr_ref, ci_ref):
    tr = twr_ref[0]
    ti = twi_ref[0]
    cr = cr_ref[...]
    ci = ci_ref[...]
    mr = cr * tr - ci * ti
    mi = ci * tr + cr * ti
    return mr, mi


def _fftb_filter_kernel(a_ref, twr_ref, twi_ref, cr_ref, ci_ref, o_ref, *, scale):
    mr, mi = _stage_b_matrix(twr_ref, twi_ref, cr_ref, ci_ref)
    fb = jnp.concatenate([jnp.concatenate([mr, -mi], axis=1),
                          jnp.concatenate([mi, mr], axis=1)], axis=0).astype(BF16)
    nparts = a_ref.shape[0]
    c = a_ref.shape[2] * LANES
    hs = []
    for p in range(nparts):
        ar, ai = _unpack_pair(_load_stage_b(a_ref, p))
        a = jnp.concatenate([ar, ai], axis=0).astype(BF16)
        hs.append(jnp.dot(fb, a, preferred_element_type=F32))
    for order in range(nparts // 2):
        fwd, bwd = hs[order], hs[nparts // 2 + order]
        o_ref[0, 0, :, order * c:(order + 1) * c] = (fwd[:FFT_N1] + bwd[:FFT_N1]) * scale
        o_ref[0, 1, :, order * c:(order + 1) * c] = (fwd[FFT_N1:] - bwd[FFT_N1:]) * scale


def _fft_stage_b_filter(ap, tabs, scale):
    k2p = tabs["k2p"]
    nparts, ngrp, nct = ap.shape[:3]
    c = nct * LANES
    return pl.pallas_call(
        functools.partial(_fftb_filter_kernel, scale=scale),
        grid=(k2p,),
        in_specs=[pl.BlockSpec((nparts, ngrp, nct, SUBLANES, LANES), lambda k: (0, 0, 0, k, 0)),
                  pl.BlockSpec((1, 1, FFT_N1), lambda k: (k, 0, 0)),
                  pl.BlockSpec((1, 1, FFT_N1), lambda k: (k, 0, 0)),
                  pl.BlockSpec((FFT_N1, FFT_N1), lambda k: (0, 0)),
                  pl.BlockSpec((FFT_N1, FFT_N1), lambda k: (0, 0))],
        out_specs=pl.BlockSpec((1, 2, FFT_N1, nparts // 2 * c), lambda k: (k, 0, 0, 0)),
        out_shape=jax.ShapeDtypeStruct((k2p, 2, FFT_N1, nparts // 2 * c), F32),
        compiler_params=_cparams(1),
        name="fft_stage_b_filter",
    )(ap, tabs["twr"], tabs["twi"], tabs["cr"], tabs["ci"])


def _fftb_conv_kernel(a_ref, g_ref, twr_ref, twi_ref, cr_ref, ci_ref, o_ref):
    mr, mi = _stage_b_matrix(twr_ref, twi_ref, cr_ref, ci_ref)
    fb = jnp.concatenate([jnp.concatenate([mr, -mi], axis=1),
                          jnp.concatenate([mi, mr], axis=1)], axis=0).astype(BF16)
    mrt, mit = mr.T, mi.T
    fbt = jnp.concatenate([jnp.concatenate([mrt, mit], axis=1),
                           jnp.concatenate([-mit, mrt], axis=1)], axis=0).astype(BF16)
    gr = g_ref[0, 0]
    gi = g_ref[0, 1]
    ngrp, nct = a_ref.shape[1], a_ref.shape[2]
    for b in range(a_ref.shape[0]):
        ar, ai = _unpack_pair(_load_stage_b(a_ref, b))
        a = jnp.concatenate([ar, ai], axis=0).astype(BF16)
        xh = jnp.dot(fb, a, preferred_element_type=F32)
        xr, xi = xh[:FFT_N1], xh[FFT_N1:]
        y = jnp.concatenate([xr * gr - xi * gi, xr * gi + xi * gr], axis=0).astype(BF16)
        cc = jnp.dot(fbt, y, preferred_element_type=F32)
        w = _pack_pair(cc[:FFT_N1], cc[FFT_N1:])
        for g in range(ngrp):
            for ct in range(nct):
                o_ref[b, g, ct] = w[g * SUBLANES:(g + 1) * SUBLANES, ct * LANES:(ct + 1) * LANES]


def _fft_stage_b_conv(ap, g, order, tabs):
    bsz, ngrp, nct = ap.shape[:3]
    k2p = tabs["k2p"]
    c = nct * LANES
    blk = pl.BlockSpec((bsz, ngrp, nct, SUBLANES, LANES), lambda k: (0, 0, 0, k, 0))
    return pl.pallas_call(
        _fftb_conv_kernel,
        grid=(k2p,),
        in_specs=[blk,
                  pl.BlockSpec((1, 2, FFT_N1, c), lambda k: (k, 0, 0, order)),
                  pl.BlockSpec((1, 1, FFT_N1), lambda k: (k, 0, 0)),
                  pl.BlockSpec((1, 1, FFT_N1), lambda k: (k, 0, 0)),
                  pl.BlockSpec((FFT_N1, FFT_N1), lambda k: (0, 0)),
                  pl.BlockSpec((FFT_N1, FFT_N1), lambda k: (0, 0))],
        out_specs=blk,
        out_shape=jax.ShapeDtypeStruct(ap.shape, U32),
        compiler_params=_cparams(1),
        name="fft_stage_b_conv",
    )(ap, g, tabs["twr"], tabs["twi"], tabs["cr"], tabs["ci"])


def _fftc_kernel(c_ref, gc_ref, u_ref, gate_ref, d_ref, o_ref, *, k2p, c):
    for j in range(SUBLANES):
        w = jnp.concatenate([c_ref[0, 0, ct, pl.ds(j, k2p, stride=SUBLANES), :] for ct in range(c // LANES)],
                            axis=1)
        re, im = _unpack_pair(w)
        cm = jnp.concatenate([re, im], axis=0).astype(BF16)
        y = jnp.dot(gc_ref[...], cm, preferred_element_type=F32)
        sl = slice(j * c, (j + 1) * c)
        o_ref[0, :, sl] = gate_ref[0, :, sl] * (y + d_ref[...] * u_ref[0, :, sl])


def _fft_stage_c(cp, gc, uf, gatef, d_row):
    bsz, ngrp, nct = cp.shape[:3]
    n2h = uf.shape[1]
    c = nct * LANES
    k2p = gc.shape[1] // 2
    fblk = pl.BlockSpec((1, n2h, SUBLANES * c), lambda b, g: (b, 0, g))
    return pl.pallas_call(
        functools.partial(_fftc_kernel, k2p=k2p, c=c),
        grid=(bsz, ngrp),
        in_specs=[pl.BlockSpec((1, 1, nct, k2p * SUBLANES, LANES), lambda b, g: (b, g, 0, 0, 0)),
                  pl.BlockSpec((n2h, 2 * k2p), lambda b, g: (0, 0)),
                  fblk, fblk,
                  pl.BlockSpec((1, c), lambda b, g: (0, 0))],
        out_specs=fblk,
        out_shape=jax.ShapeDtypeStruct(uf.shape, F32),
        compiler_params=_cparams(2),
        name="fft_stage_c",
    )(cp, gc, uf, gatef, d_row)


def _hyena_mix(hv, hx1, hx2, filt, skip, seq):
    c = skip.shape[1]
    tabs = _fft_tables(seq)
    g = _fft_stage_b_filter(_fft_stage_a(filt, tabs["fa"], c), tabs, 1.0 / tabs["n"])

    def conv(u, gate, order):
        cp = _fft_stage_b_conv(_fft_stage_a(u, tabs["fa"], c), g, order, tabs)
        return _fft_stage_c(cp, tabs["gc"], u, gate, skip[order].reshape(1, c))

    return conv(conv(hv, hx1, 0), hx2, 1)


def _qkprep_kernel(q_ref, k_ref, v_ref, qn_ref, kn_ref, oqt_ref, ok_ref, ovt_ref, *, tt, seq):
    i = pl.program_id(0)
    t = (i * tt + lax.broadcasted_iota(I32, (tt, HEAD_DIM), 0)) % seq
    lane = lax.broadcasted_iota(I32, (tt, HEAD_DIM), 1)
    j = lane % (HEAD_DIM // 2)
    quarter = HEAD_DIM // 4
    f = (j % quarter).astype(F32)
    inv = jnp.exp(f * (-2.0 * math.log(ROPE_THETA) / (HEAD_DIM // 2)))
    pos = jnp.where(j < quarter, t // GRID_W, t % GRID_W).astype(F32)
    ang = pos * inv
    cosv = jnp.cos(ang)
    sinv = jnp.sin(ang)
    sin_signed = jnp.where(lane < HEAD_DIM // 2, -sinv, sinv)

    def norm_rope(x, g, scale):
        ms = jnp.mean(x * x, axis=-1, keepdims=True)
        xn = x * lax.rsqrt(ms + NORM_EPS) * g
        return (xn * cosv + pltpu.roll(xn, HEAD_DIM // 2, axis=1) * sin_signed) * scale

    qg = qn_ref[...]
    kg = kn_ref[...]
    q_scale = LOG2_E * HEAD_DIM ** -0.5
    for h in range(N_HEADS):
        sl = slice(h * HEAD_DIM, (h + 1) * HEAD_DIM)
        oqt_ref[h] = norm_rope(q_ref[:, sl], qg, q_scale).T.astype(BF16)
    for h in range(N_KV_HEADS):
        sl = slice(h * HEAD_DIM, (h + 1) * HEAD_DIM)
        ok_ref[:, sl] = norm_rope(k_ref[:, sl], kg, 1.0).astype(BF16)
        ovt_ref[h, :HEAD_DIM, :] = v_ref[:, sl].T.astype(BF16)
        extra = lax.broadcasted_iota(I32, (V_ROWS - HEAD_DIM, tt), 0)
        ovt_ref[h, HEAD_DIM:, :] = jnp.where(extra == 0, 1.0, 0.0).astype(BF16)


def _qk_prep(proj, qn, kn, seq):
    n = proj.shape[0]
    tt = min(512, seq)
    qw = N_HEADS * HEAD_DIM
    kw = N_KV_HEADS * HEAD_DIM
    q_off = (3 * HY_WIDTH) // qw
    k_off = (3 * HY_WIDTH + qw) // kw
    return pl.pallas_call(
        functools.partial(_qkprep_kernel, tt=tt, seq=seq),
        grid=(n // tt,),
        in_specs=[pl.BlockSpec((tt, qw), lambda i: (i, q_off)),
                  pl.BlockSpec((tt, kw), lambda i: (i, k_off)),
                  pl.BlockSpec((tt, kw), lambda i: (i, k_off + 1)),
                  pl.BlockSpec((1, HEAD_DIM), lambda i: (0, 0)),
                  pl.BlockSpec((1, HEAD_DIM), lambda i: (0, 0))],
        out_specs=[pl.BlockSpec((N_HEADS, HEAD_DIM, tt), lambda i: (0, 0, i)),
                   pl.BlockSpec((tt, kw), lambda i: (i, 0)),
                   pl.BlockSpec((N_KV_HEADS, V_ROWS, tt), lambda i: (0, 0, i))],
        out_shape=[jax.ShapeDtypeStruct((N_HEADS, HEAD_DIM, n), BF16),
                   jax.ShapeDtypeStruct((n, kw), BF16),
                   jax.ShapeDtypeStruct((N_KV_HEADS, V_ROWS, n), BF16)],
        compiler_params=_cparams(1),
        name="qk_prep",
    )(proj, proj, proj, qn, kn)


def _flash_kernel(qt_ref, k_ref, vt_ref, o_ref, acc_sc, s_sc, mc_sc, *, tq, tk, seq, unroll):
    nq = Q_PER_KV * tq
    nk = seq // tk
    qt = jnp.concatenate([qt_ref[g] for g in range(Q_PER_KV)], axis=1)
    acc_sc[...] = jnp.zeros(acc_sc.shape, F32)

    def scores(c, slot):
        start = pl.multiple_of(c * tk, tk)
        s = jnp.dot(k_ref[pl.ds(start, tk), :], qt, preferred_element_type=F32)
        s_sc[slot] = s
        mc_sc[slot] = jnp.max(s, axis=0, keepdims=True)

    scores(0, 0)

    def trip(i, m, last):
        for u in range(unroll):
            c = i * unroll + u
            if not (last and u == unroll - 1):
                scores(c + 1, (u + 1) % 2)
            s = s_sc[u % 2]
            mc = mc_sc[u % 2]
            start = pl.multiple_of(c * tk, tk)
            vc = vt_ref[0, :, pl.ds(start, tk)]
            p = jnp.exp2(s - mc).astype(BF16)
            oc = jnp.dot(vc, p, preferred_element_type=F32)
            m_new = jnp.maximum(m, mc)
            acc_sc[...] = acc_sc[...] * jnp.exp2(m - m_new) + oc * jnp.exp2(mc - m_new)
            m = m_new
        return m

    ntrips = nk // unroll
    m = lax.fori_loop(0, ntrips - 1, lambda i, m: trip(i, m, False), jnp.full((1, nq), -jnp.inf, F32))
    trip(ntrips - 1, m, True)
    o = acc_sc[:HEAD_DIM, :] / acc_sc[HEAD_DIM:HEAD_DIM + 1, :]
    for g in range(Q_PER_KV):
        o_ref[:, g * HEAD_DIM:(g + 1) * HEAD_DIM] = o[:, g * tq:(g + 1) * tq].T


def _flash_attention(qt, k, vt, bsz, seq):
    n = bsz * seq
    tq = min(256, seq)
    tk = min(512, seq // 2)
    unroll = 8 if seq // tk >= 16 else (2 if seq // tk >= 2 else 1)
    gw = Q_PER_KV * HEAD_DIM
    nqb = seq // tq
    return pl.pallas_call(
        functools.partial(_flash_kernel, tq=tq, tk=tk, seq=seq, unroll=unroll),
        grid=(bsz, N_KV_HEADS, nqb),
        in_specs=[pl.BlockSpec((Q_PER_KV, HEAD_DIM, tq), lambda b, h, i: (h, 0, b * nqb + i)),
                  pl.BlockSpec((seq, HEAD_DIM), lambda b, h, i: (b, h)),
                  pl.BlockSpec((1, V_ROWS, seq), lambda b, h, i: (h, 0, b))],
        out_specs=pl.BlockSpec((tq, gw), lambda b, h, i: (b * nqb + i, h)),
        out_shape=jax.ShapeDtypeStruct((n, N_HEADS * HEAD_DIM), F32),
        scratch_shapes=[pltpu.VMEM((V_ROWS, Q_PER_KV * tq), F32),
                        pltpu.VMEM((2, tk, Q_PER_KV * tq), F32),
                        pltpu.VMEM((2, 1, Q_PER_KV * tq), F32)],
        compiler_params=_cparams(3),
        name="flash_attention",
    )(qt, k, vt)


def _layer_norm(y, g, b):
    mu = jnp.mean(y, axis=-1, keepdims=True)
    yc = y - mu
    var = jnp.mean(yc * yc, axis=-1, keepdims=True)
    return yc * lax.rsqrt(var + NORM_EPS) * g + b


def _merge_kernel(yh_ref, ya_ref, x_ref, gh_ref, ga_ref, wo_ref, lg_ref, lb_ref, wr_ref, br_ref,
                  x1_ref, aff_ref):
    hw = yh_ref.shape[1]
    yh = yh_ref[...]
    ya = ya_ref[...]
    mh = yh * lax.rsqrt(jnp.mean(yh * yh, axis=-1, keepdims=True) + NORM_EPS) * gh_ref[...]
    ma = ya * lax.rsqrt(jnp.mean(ya * ya, axis=-1, keepdims=True) + NORM_EPS) * ga_ref[...]
    y = (jnp.dot(mh.astype(BF16), wo_ref[:hw, :], preferred_element_type=F32)
         + jnp.dot(ma.astype(BF16), wo_ref[hw:, :], preferred_element_type=F32)
         + DN_ALPHA * x_ref[...])
    x1 = _layer_norm(y, lg_ref[...], lb_ref[...])
    x1_ref[...] = x1
    x_hi = x1.astype(BF16)
    x_lo = (x1 - x_hi.astype(F32)).astype(BF16)
    both = jnp.dot(x_hi, wr_ref[...], preferred_element_type=F32)
    logits = (both[:, :LANES] + both[:, LANES:]
              + jnp.dot(x_lo, wr_ref[:, :LANES], preferred_element_type=F32) + br_ref[...])
    e = jnp.exp(logits - jnp.max(logits, axis=-1, keepdims=True))
    aff_ref[...] = e / jnp.sum(e, axis=-1, keepdims=True)


def _merge(y_hy, y_att, x2, g_hy, g_attn, wo_bf, ln_g, ln_b, wr_pad, br_pad):
    n, d = x2.shape
    hw = y_hy.shape[1]
    tm = min(256, n)
    row = lambda w: pl.BlockSpec((1, w), lambda i: (0, 0))
    return pl.pallas_call(
        _merge_kernel,
        grid=(n // tm,),
        in_specs=[pl.BlockSpec((tm, hw), lambda i: (i, 0)),
                  pl.BlockSpec((tm, d - hw), lambda i: (i, 0)),
                  pl.BlockSpec((tm, d), lambda i: (i, 0)),
                  row(hw), row(d - hw),
                  pl.BlockSpec((d, d), lambda i: (0, 0)),
                  row(d), row(d),
                  pl.BlockSpec((d, 2 * LANES), lambda i: (0, 0)),
                  row(LANES)],
        out_specs=[pl.BlockSpec((tm, d), lambda i: (i, 0)),
                   pl.BlockSpec((tm, LANES), lambda i: (i, 0))],
        out_shape=[jax.ShapeDtypeStruct((n, d), F32), jax.ShapeDtypeStruct((n, LANES), F32)],
        compiler_params=_cparams(1),
        name="merge_ln_router",
    )(y_hy, y_att, x2, g_hy, g_attn, wo_bf, ln_g, ln_b, wr_pad, br_pad)


def _select_kernel(a_ref, idx_ref, gslot_ref, pos_ref, excl_ref, *, cap):
    a = a_ref[0]
    ne, nr, _ = a.shape
    bits = lax.bitcast_convert_type(a, I32)

    def count(mask):
        return jnp.sum(jnp.sum(mask.astype(F32), axis=2, keepdims=True), axis=1, keepdims=True)

    thr = jnp.zeros((ne, 1, 1), I32)
    for bit in range(30, -1, -1):
        cand = thr | (1 << bit)
        thr = jnp.where(count(bits >= cand) >= cap, cand, thr)

    ii = lax.broadcasted_iota(I32, (LANES, LANES), 0)
    jj = lax.broadcasted_iota(I32, (LANES, LANES), 1)
    upper = (ii <= jj).astype(BF16)
    ri = lax.broadcasted_iota(I32, (ne, nr, nr), 1)
    rj = lax.broadcasted_iota(I32, (ne, nr, nr), 2)
    lower_strict = (rj < ri).astype(BF16)

    def prefix(mask):
        mb = mask.astype(BF16).reshape(ne * nr, LANES)
        pin = jnp.dot(mb, upper, preferred_element_type=F32).reshape(ne, nr, LANES)
        tot = pin[:, :, LANES - 1:LANES]
        totb = jnp.broadcast_to(tot, (ne, nr, LANES)).astype(BF16)
        off = jnp.einsum("ers,esl->erl", lower_strict, totb, preferred_element_type=F32)
        return pin, tot, off[:, :, 0:1]

    gt = bits > thr
    eq = bits == thr
    need = cap - count(gt)
    pin_eq, _, off_eq = prefix(eq)
    eq_rank = pin_eq + off_eq - eq.astype(F32)
    chosen = gt | (eq & (eq_rank < need))
    pin, tot, off = prefix(chosen)
    chosen_f = chosen.astype(F32)
    excl = pin + off - chosen_f
    excl_ref[0] = excl
    pos_ref[0] = jnp.where(chosen, excl, -1.0)

    s_row = lax.broadcasted_iota(I32, (nr, cap), 1).astype(F32)
    r_col = lax.broadcasted_iota(I32, (nr, cap), 0).astype(F32)
    j_col = lax.broadcasted_iota(I32, (LANES, cap), 0).astype(F32)
    for e in range(ne):
        lo = off[e]
        hi = lo + tot[e]
        in_row = ((lo <= s_row) & (s_row < hi)).astype(F32)
        in_row_b = in_row.astype(BF16)
        row_of_s = jnp.sum(in_row * r_col, axis=0, keepdims=True)
        s_local = s_row[0:1] - jnp.sum(in_row * lo, axis=0, keepdims=True)
        pin_t = pin[e].T.astype(BF16)
        g = jnp.dot(pin_t, in_row_b, preferred_element_type=F32)
        lane_of_s = jnp.sum((g <= s_local).astype(F32), axis=0, keepdims=True)
        idx_ref[0, e:e + 1, :] = (row_of_s * LANES + lane_of_s).astype(I32)
        rem = a[e]
        a_of_s = jnp.zeros((LANES, cap), F32)
        for _ in range(3):
            part = rem.astype(BF16)
            rem = rem - part.astype(F32)
            a_of_s = a_of_s + jnp.dot(part.astype(F32).T.astype(BF16), in_row_b, preferred_element_type=F32)
        gslot_ref[0, e:e + 1, :] = jnp.sum(jnp.where(j_col == lane_of_s, a_of_s, 0.0), axis=0, keepdims=True)


def _select(aff4, cap):
    bsz, ne, nr, _ = aff4.shape
    blk = pl.BlockSpec((1, ne, nr, LANES), lambda b: (b, 0, 0, 0))
    slot = pl.BlockSpec((1, ne, cap), lambda b: (b, 0, 0))
    return pl.pallas_call(
        functools.partial(_select_kernel, cap=cap),
        grid=(bsz,),
        in_specs=[blk],
        out_specs=[slot, slot, blk, blk],
        out_shape=[jax.ShapeDtypeStruct((bsz, ne, cap), I32),
                   jax.ShapeDtypeStruct((bsz, ne, cap), F32),
                   jax.ShapeDtypeStruct(aff4.shape, F32),
                   jax.ShapeDtypeStruct(aff4.shape, F32)],
        compiler_params=_cparams(1),
        name="expert_choice_select",
    )(aff4)


def _expert_kernel(idx_ref, idxn_ref, x_hbm, g_ref, wg_ref, wu_ref, wd_ref, o_ref, xg_a, xg_b, sem,
                   *, tc, nchunk, seq, ff, npair, bsz, nsteps):
    s = pl.program_id(0)

    def issue(ids_ref, first, step, buf, k):
        base = ((step // npair) % bsz) * seq
        for r in range(tc):
            pltpu.make_async_copy(x_hbm.at[pl.ds(base + ids_ref[0, 0, first + r], 1)],
                                  buf.at[pl.ds(r, 1)], sem.at[k]).start()

    def wait_rows(buf, k):
        pltpu.make_async_copy(x_hbm.at[pl.ds(0, tc)], buf, sem.at[k]).wait()

    def ffn(buf, first):
        xb = buf[...].astype(BF16)
        gate = jnp.dot(xb, wg_ref[0], preferred_element_type=F32)
        h = gate * jax.nn.sigmoid(gate) * jnp.dot(xb, wu_ref[0], preferred_element_type=F32)
        y = jnp.dot(h.astype(BF16), wd_ref[0], preferred_element_type=F32)
        o_ref[0, 0, first:first + tc, :] = (y * g_ref[0, first:first + tc, :]).astype(BF16)

    @pl.when(s == 0)
    def _():
        issue(idx_ref, 0, s, xg_a, 0)

    bufs = (xg_a, xg_b)
    for k in range(nchunk):
        wait_rows(bufs[k % 2], k % 2)
        if k + 1 < nchunk:
            issue(idx_ref, (k + 1) * tc, s, bufs[(k + 1) % 2], (k + 1) % 2)
        else:
            issue(idxn_ref, 0, jnp.minimum(s + 1, nsteps - 1), xg_a, 0)
        ffn(bufs[k % 2], k * tc)

    @pl.when(s == nsteps - 1)
    def _():
        wait_rows(xg_a, 0)


def _experts(idx, gslot, x1, wg_bf, wu_bf, wd_bf, bsz, seq):
    _, ne, cap = idx.shape
    d = x1.shape[1]
    ff = wd_bf.shape[1]
    tc = min(256, cap // 2)
    nchunk = 4 if cap % (4 * tc) == 0 else 2
    rows = nchunk * tc
    npair = cap // rows
    nsteps = ne * bsz * npair
    idx_s = idx.transpose(1, 0, 2).reshape(nsteps, 1, rows)
    g_s = gslot.transpose(1, 0, 2).reshape(nsteps, rows, 1)
    out_map = lambda s: ((s // npair) % bsz, s // (bsz * npair), s % npair, 0)
    wmap = lambda s: (s // (bsz * npair), 0, 0)
    return pl.pallas_call(
        functools.partial(_expert_kernel, tc=tc, nchunk=nchunk, seq=seq, ff=ff, npair=npair, bsz=bsz,
                          nsteps=nsteps),
        grid=(nsteps,),
        in_specs=[pl.BlockSpec((1, 1, rows), lambda s: (s, 0, 0), memory_space=pltpu.SMEM),
                  pl.BlockSpec((1, 1, rows), lambda s: (jnp.minimum(s + 1, nsteps - 1), 0, 0),
                               memory_space=pltpu.SMEM),
                  pl.BlockSpec(memory_space=pl.ANY),
                  pl.BlockSpec((1, rows, 1), lambda s: (s, 0, 0)),
                  pl.BlockSpec((1, d, ff), wmap),
                  pl.BlockSpec((1, d, ff), wmap),
                  pl.BlockSpec((1, ff, d), wmap)],
        out_specs=pl.BlockSpec((1, 1, rows, d), out_map),
        out_shape=jax.ShapeDtypeStruct((bsz, ne, cap, d), BF16),
        scratch_shapes=[pltpu.VMEM((tc, d), F32), pltpu.VMEM((tc, d), F32), pltpu.SemaphoreType.DMA((2,))],
        compiler_params=_cparams(1),
        name="expert_ffn",
    )(idx_s, idx_s, x1, g_s, wg_bf, wu_bf, wd_bf)


def _combine_kernel(w0_ref, y_hbm, x1_ref, pos_ref, lg_ref, lb_ref, o_ref, ybuf, xbuf, acc_ref, sem, xsem,
                    *, tb_tokens, win, cap, ne, ntb):
    s = pl.program_id(0)
    nsteps = pl.num_programs(0)
    slot = s % 2

    def window_copies(step, to_slot):
        bb = step // ntb
        wbase = (bb * (ntb + 1) + step % ntb) * ne
        copies, firsts = [], []
        for e in range(ne):
            sa = jnp.minimum((w0_ref[wbase + e] // BF16_ROWS) * BF16_ROWS, cap - win)
            sa = pl.multiple_of(sa, BF16_ROWS)
            firsts.append(sa)
            copies.append(pltpu.make_async_copy(y_hbm.at[bb, e, pl.ds(sa, win)],
                                                ybuf.at[to_slot, pl.ds(e * win, win)], sem.at[to_slot, e]))
        return copies, firsts

    @pl.when(s == 0)
    def _():
        for cp in window_copies(s, 0)[0]:
            cp.start()

    @pl.when(s + 1 < nsteps)
    def _():
        for cp in window_copies(s + 1, 1 - slot)[0]:
            cp.start()

    b = s // ntb
    base = (b * (ntb + 1) + s % ntb) * ne
    copies, starts = window_copies(s, slot)
    lane = lax.broadcasted_iota(I32, (tb_tokens, win), 1).astype(F32)
    pack = LANES // win
    glane = lax.broadcasted_iota(I32, (tb_tokens, LANES), 1).astype(F32)
    onehots = []
    for grp in range(ne // pack):
        hit = None
        for q in range(pack):
            e = grp * pack + q
            rel = pos_ref[0, :, e:e + 1] - starts[e].astype(F32) + float(q * win)
            m = (rel == glane) & (glane >= float(q * win)) & (glane < float((q + 1) * win))
            hit = m if hit is None else hit | m
        onehots.append(hit.astype(BF16))
    p = jnp.concatenate(onehots, axis=1)
    for cp in copies:
        cp.wait()
    acc_ref[...] = DN_ALPHA * x1_ref[...] + jnp.dot(p, ybuf[slot], preferred_element_type=F32)

    for e in range(ne):
        first_end = starts[e] + win
        n_extra = jnp.maximum(w0_ref[base + ne + e] - first_end + win - 1, 0) // win

        def extra(w, carry, e=e, first_end=first_end):
            nominal = first_end + w * win
            st = pl.multiple_of(jnp.minimum(nominal, cap - win), BF16_ROWS)
            cp = pltpu.make_async_copy(y_hbm.at[b, e, pl.ds(st, win)], xbuf, xsem)
            cp.start()
            cp.wait()
            pe = pos_ref[0, :, e:e + 1]
            oh = ((pe - st.astype(F32) == lane) & (pe >= nominal.astype(F32))).astype(BF16)
            acc_ref[...] += jnp.dot(oh, xbuf[...], preferred_element_type=F32)
            return carry

        lax.fori_loop(0, n_extra, extra, 0)
    o_ref[...] = _layer_norm(acc_ref[...], lg_ref[...], lb_ref[...])


def _combine(y, x1, pos_t, w0, ln_g, ln_b, bsz, seq, tbt):
    _, ne, cap, d = y.shape
    win = min(LANES // 2, cap)
    ntb = seq // tbt
    grid_spec = pltpu.PrefetchScalarGridSpec(
        num_scalar_prefetch=1,
        grid=(bsz * ntb,),
        in_specs=[pl.BlockSpec(memory_space=pl.ANY),
                  pl.BlockSpec((tbt, d), lambda s, w: (s, 0)),
                  pl.BlockSpec((1, tbt, ne), lambda s, w: (s // ntb, s % ntb, 0)),
                  pl.BlockSpec((1, d), lambda s, w: (0, 0)),
                  pl.BlockSpec((1, d), lambda s, w: (0, 0))],
        out_specs=pl.BlockSpec((tbt, d), lambda s, w: (s, 0)),
        scratch_shapes=[pltpu.VMEM((2, ne * win, d), BF16), pltpu.VMEM((win, d), BF16),
                        pltpu.VMEM((tbt, d), F32),
                        pltpu.SemaphoreType.DMA((2, ne)), pltpu.SemaphoreType.DMA(())],
    )
    return pl.pallas_call(
        functools.partial(_combine_kernel, tb_tokens=tbt, win=win, cap=cap, ne=ne, ntb=ntb),
        grid_spec=grid_spec,
        out_shape=jax.ShapeDtypeStruct((bsz * seq, d), F32),
        compiler_params=_cparams(1),
        name="moe_combine_ln",
    )(w0, y, x1, pos_t, ln_g, ln_b)


def _deinterleave(width):
    idx = np.arange(width).reshape(-1, HEAD_DIM // 2, 2)
    return np.concatenate([idx[:, :, 0], idx[:, :, 1]], axis=1).reshape(-1)


def kernel(x, w_in, b_in, hy_conv_w, hy_conv_b, hy_ffn_w1, hy_ffn_b1, hy_sin_f1, hy_ffn_w2, hy_ffn_b2, hy_sin_f2, hy_ffn_w3, hy_decay, hy_skip, q_norm, k_norm, g_hy, g_attn, w_out, ln1_g, ln1_b, w_router, b_router, w_gate, w_up, w_down, ln2_g, ln2_b):
    bsz, seq, d = x.shape
    n = bsz * seq
    x2 = x.reshape(n, d)
    s1 = 3 * HY_WIDTH
    qw = N_HEADS * HEAD_DIM
    kw = N_KV_HEADS * HEAD_DIM
    perm = np.concatenate([np.arange(s1), s1 + _deinterleave(qw), s1 + qw + _deinterleave(kw),
                           np.arange(s1 + qw + kw, w_in.shape[1])])
    w_in_bf = w_in[:, perm].astype(BF16)
    b_in_p = b_in[perm].reshape(1, -1)
    hd_perm = _deinterleave(HEAD_DIM)

    proj = _in_projection(x2, w_in_bf, b_in_p)

    conv_b = hy_conv_b.reshape(1, -1)
    hv, hx1, hx2 = [_short_conv(proj, hy_conv_w, conv_b, part, bsz, seq) for part in range(3)]
    filt = _hyena_filters(seq, hy_ffn_w1, hy_ffn_b1, hy_sin_f1, hy_ffn_w2, hy_ffn_b2, hy_sin_f2,
                          hy_ffn_w3, hy_decay)
    y_hy = _hyena_mix(hv, hx1, hx2, filt, hy_skip, seq).reshape(n, HY_WIDTH)

    qt, kb, vt = _qk_prep(proj, q_norm[hd_perm].reshape(1, HEAD_DIM), k_norm[hd_perm].reshape(1, HEAD_DIM), seq)
    y_att = _flash_attention(qt, kb, vt, bsz, seq)

    ne = w_router.shape[1]
    wr_hi = w_router.astype(BF16)
    wr_lo = (w_router - wr_hi.astype(F32)).astype(BF16)
    wr_pair = (jnp.zeros((d, 2 * LANES), BF16).at[:, :ne].set(wr_hi).at[:, LANES:LANES + ne].set(wr_lo))
    br_pad = jnp.full((1, LANES), -1e30, F32).at[0, :ne].set(b_router)
    x1, aff = _merge(y_hy, y_att, x2, g_hy.reshape(1, -1), g_attn.reshape(1, -1),
                     w_out.astype(BF16), ln1_g.reshape(1, d), ln1_b.reshape(1, d), wr_pair, br_pad)

    cap = CAPACITY_FACTOR * seq // ne
    aff4 = aff[:, :ne].reshape(bsz, seq, ne).transpose(0, 2, 1).reshape(bsz, ne, seq // LANES, LANES)
    idx, gslot, pos, excl = _select(aff4, cap)
    y = _experts(idx, gslot, x1, w_gate.astype(BF16), w_up.astype(BF16), w_down.astype(BF16), bsz, seq)
    pos_t = pos.reshape(bsz, ne, seq).transpose(0, 2, 1)
    tbt = min(256, cap)
    before = excl.reshape(bsz, ne, seq // tbt, tbt)[:, :, :, 0].transpose(0, 2, 1)
    w0 = jnp.concatenate([before, jnp.full((bsz, 1, ne), cap, F32)], axis=1).reshape(-1).astype(I32)
    out = _combine(y, x1, pos_t, w0, ln2_g.reshape(1, d), ln2_b.reshape(1, d), bsz, seq, tbt)
    return out.reshape(bsz, seq, d)
```
